```python
import math
import jax, jax.numpy as jnp
from jax import lax
import numpy as np

D_MODEL = 1024
BATCH = 8
SEQ = 8192
DEPTH = 1

MLA_HEADS = 8
Q_LORA_RANK = 256
KV_LORA_RANK = 128
QK_NOPE_DIM = 64
QK_ROPE_DIM = 32
V_HEAD_DIM = 64
ROPE_THETA = 10000.0
Q_BLOCK = 128
SSM_GROUP = 16
SSM_GROUPS = 32
SSM_WIDTH = SSM_GROUP * SSM_GROUPS
SSM_STATE = 64
DT_MIN = 0.001
DT_MAX = 0.1
D_FF = 2816
FFN_RES = 0.5
EPS = 1e-6
N_BRANCH = 2
MAX_POS_OFFSET = 4096

OFF_CQ = 0
OFF_CKV = OFF_CQ + Q_LORA_RANK
OFF_KR = OFF_CKV + KV_LORA_RANK
OFF_U = OFF_KR + QK_ROPE_DIM
OFF_GA = OFF_U + SSM_WIDTH
OFF_GS = OFF_GA + D_MODEL
IN_PROJ = OFF_GS + D_MODEL

kernel_name = "hybrid_mla_s5_macaron_encoder"


def _rmsnorm(x, g):
    x32 = x.astype(jnp.float32)
    y = x32 * lax.rsqrt(jnp.mean(x32 * x32, axis=-1, keepdims=True) + EPS)
    return (y * g.astype(jnp.float32)).astype(x.dtype)


def _swiglu(x, w_gate, w_up, w_down):
    return (jax.nn.silu(x @ w_gate) * (x @ w_up)) @ w_down


def _rope_tables(positions):
    inv_freq = ROPE_THETA ** (-jnp.arange(0, QK_ROPE_DIM, 2, dtype=jnp.float32) / QK_ROPE_DIM)
    ang = positions.astype(jnp.float32)[..., None] * inv_freq
    return jnp.cos(ang), jnp.sin(ang)


def _apply_rope(t, cos, sin):
    half = t.shape[-1] // 2
    cos = cos.astype(t.dtype)
    sin = sin.astype(t.dtype)
    t1, t2 = t[..., :half], t[..., half:]
    return jnp.concatenate([t1 * cos - t2 * sin, t2 * cos + t1 * sin], axis=-1)


def _mla_attention(q_nope, q_rope, k_nope, k_rope, v):
    b, s, h, _ = q_nope.shape
    dv = v.shape[-1]
    n_blk = s // Q_BLOCK
    scale = (QK_NOPE_DIM + QK_ROPE_DIM) ** -0.5

    def blocks(t):
        return t.reshape((b, n_blk, Q_BLOCK) + t.shape[2:]).swapaxes(0, 1)

    def attend(qs):
        qn, qr = qs
        sc = (jnp.einsum('bqhd,bkhd->bhqk', qn, k_nope)
              + jnp.einsum('bqhr,bkr->bhqk', qr, k_rope))
        p = jax.nn.softmax(sc.astype(jnp.float32) * scale, axis=-1).astype(v.dtype)
        return jnp.einsum('bhqk,bkhd->bqhd', p, v)

    o = lax.map(attend, (blocks(q_nope), blocks(q_rope)))
    return o.swapaxes(0, 1).reshape(b, s, h * dv)


def _ssm_direction(u, lam_re, lam_im, log_dt, b_re, b_im, c_re, c_im, reverse):
    dt = jnp.exp(log_dt)[:, None]
    mag = jnp.exp(lam_re * dt)
    lb_re = mag * jnp.cos(lam_im * dt)
    lb_im = mag * jnp.sin(lam_im * dt)
    den = lam_re * lam_re + lam_im * lam_im
    nr = lb_re - 1.0
    k_re = (nr * lam_re + lb_im * lam_im) / den
    k_im = (lb_im * lam_re - nr * lam_im) / den
    bb_re = k_re[..., None] * b_re - k_im[..., None] * b_im
    bb_im = k_re[..., None] * b_im + k_im[..., None] * b_re
    x_re = jnp.einsum('sbgi,gpi->sbgp', u, bb_re)
    x_im = jnp.einsum('sbgi,gpi->sbgp', u, bb_im)
    seq = u.shape[0]
    a_re = jnp.broadcast_to(lb_re, (seq, 1) + lb_re.shape)
    a_im = jnp.broadcast_to(lb_im, (seq, 1) + lb_im.shape)

    def combine(e1, e2):
        ar1, ai1, xr1, xi1 = e1
        ar2, ai2, xr2, xi2 = e2
        ar = ar2 * ar1 - ai2 * ai1
        ai = ar2 * ai1 + ai2 * ar1
        xr = ar2 * xr1 - ai2 * xi1 + xr2
        xi = ar2 * xi1 + ai2 * xr1 + xi2
        return (ar, ai, xr, xi)

    _, _, h_re, h_im = lax.associative_scan(combine, (a_re, a_im, x_re, x_im),
                                            reverse=reverse, axis=0)
    return (jnp.einsum('sbgp,gip->sbgi', h_re, c_re)
            - jnp.einsum('sbgp,gip->sbgi', h_im, c_im))


def setup_inputs(seed: int = 0) -> dict:
    key = jax.random.key(seed)
    ks = iter(jax.random.split(key, 48))
    f32 = jnp.float32

    def nrm(shape, fan_in):
        return jax.random.normal(next(ks), shape, f32) * (fan_in ** -0.5)

    def gain(shape):
        return 1.0 + 0.02 * jax.random.normal(next(ks), shape, f32)

    L, D, G, P, I = DEPTH, D_MODEL, SSM_GROUPS, SSM_STATE, SSM_GROUP
    x = jax.random.normal(next(ks), (BATCH, SEQ, D), f32)
    offset = jax.random.randint(next(ks), (BATCH, 1), 0, MAX_POS_OFFSET, dtype=jnp.int32)
    positions = (jnp.arange(SEQ, dtype=jnp.int32)[None, :] + offset).astype(jnp.int32)

    n_idx = jnp.arange(P, dtype=f32)
    lam_re = -0.5 + 0.01 * jax.random.normal(next(ks), (L, 2, G, P), f32)
    lam_im = jnp.pi * n_idx + 0.01 * jax.random.normal(next(ks), (L, 2, G, P), f32)
    log_dt = jax.random.uniform(next(ks), (L, 2, G), f32,
                                minval=math.log(DT_MIN), maxval=math.log(DT_MAX))

    return {
        "x": x,
        "positions": positions,
        "ffn1_norm": gain((L, D)),
        "ffn1_w_gate": nrm((L, D, D_FF), D),
        "ffn1_w_up": nrm((L, D, D_FF), D),
        "ffn1_w_down": nrm((L, D_FF, D), D_FF),
        "mix_norm": gain((L, D)),
        "w_in": nrm((L, D, IN_PROJ), D),
        "q_norm": gain((L, Q_LORA_RANK)),
        "w_uq": nrm((L, Q_LORA_RANK, MLA_HEADS * (QK_NOPE_DIM + QK_ROPE_DIM)), Q_LORA_RANK),
        "kv_norm": gain((L, KV_LORA_RANK)),
        "w_ukv": nrm((L, KV_LORA_RANK, MLA_HEADS * (QK_NOPE_DIM + V_HEAD_DIM)), KV_LORA_RANK),
        "w_o_attn": nrm((L, MLA_HEADS * V_HEAD_DIM, D), MLA_HEADS * V_HEAD_DIM),
        "ssm_lambda_re": lam_re,
        "ssm_lambda_im": lam_im,
        "ssm_log_dt": log_dt,
        "ssm_b_re": nrm((L, 2, G, P, I), 2 * I),
        "ssm_b_im": nrm((L, 2, G, P, I), 2 * I),
        "ssm_c_re": nrm((L, 2, G, I, P), 2 * P),
        "ssm_c_im": nrm((L, 2, G, I, P), 2 * P),
        "ssm_d": jax.random.normal(next(ks), (L, SSM_WIDTH), f32),
        "w_glu": nrm((L, SSM_WIDTH, 2 * SSM_WIDTH), SSM_WIDTH),
        "b_glu": 0.01 * jax.random.normal(next(ks), (L, 2 * SSM_WIDTH), f32),
        "w_o_ssm": nrm((L, SSM_WIDTH, D), SSM_WIDTH),
        "w_out": nrm((L, D, D), D),
        "ffn2_norm": gain((L, D)),
        "ffn2_w_gate": nrm((L, D, D_FF), D),
        "ffn2_w_up": nrm((L, D, D_FF), D),
        "ffn2_w_down": nrm((L, D_FF, D), D_FF),
        "final_norm": gain((D,)),
    }


def reference(x, positions, ffn1_norm, ffn1_w_gate, ffn1_w_up, ffn1_w_down, mix_norm, w_in,
              q_norm, w_uq, kv_norm, w_ukv, w_o_attn, ssm_lambda_re, ssm_lambda_im,
              ssm_log_dt, ssm_b_re, ssm_b_im, ssm_c_re, ssm_c_im, ssm_d, w_glu, b_glu,
              w_o_ssm, w_out, ffn2_norm, ffn2_w_gate, ffn2_w_up, ffn2_w_down, final_norm):
    b, s, _ = x.shape
    f32 = jnp.float32
    cos, sin = _rope_tables(positions)
    h = x
    for l in range(DEPTH):
        h = h + FFN_RES * _swiglu(_rmsnorm(h, ffn1_norm[l]), ffn1_w_gate[l], ffn1_w_up[l], ffn1_w_down[l])

        n = _rmsnorm(h, mix_norm[l])
        z = n @ w_in[l]
        c_q = z[..., OFF_CQ:OFF_CKV]
        c_kv = z[..., OFF_CKV:OFF_KR]
        k_r = z[..., OFF_KR:OFF_U]
        u = z[..., OFF_U:OFF_GA]
        g_attn = z[..., OFF_GA:OFF_GS]
        g_ssm = z[..., OFF_GS:IN_PROJ]

        q = (_rmsnorm(c_q, q_norm[l]) @ w_uq[l]).reshape(b, s, MLA_HEADS, QK_NOPE_DIM + QK_ROPE_DIM)
        q_nope = q[..., :QK_NOPE_DIM]
        q_rope = _apply_rope(q[..., QK_NOPE_DIM:], cos[:, :, None, :], sin[:, :, None, :])
        kv = (_rmsnorm(c_kv, kv_norm[l]) @ w_ukv[l]).reshape(b, s, MLA_HEADS, QK_NOPE_DIM + V_HEAD_DIM)
        k_nope = kv[..., :QK_NOPE_DIM]
        v = kv[..., QK_NOPE_DIM:]
        k_rope = _apply_rope(k_r, cos, sin)
        attn = _mla_attention(q_nope, q_rope, k_nope, k_rope, v)

        u32 = u.astype(f32)
        ug = u32.reshape(b, s, SSM_GROUPS, SSM_GROUP).swapaxes(0, 1)
        y_dirs = []
        for direction, rev in ((0, False), (1, True)):
            y_dirs.append(_ssm_direction(
                ug,
                ssm_lambda_re[l, direction].astype(f32), ssm_lambda_im[l, direction].astype(f32),
                ssm_log_dt[l, direction].astype(f32),
                ssm_b_re[l, direction].astype(f32), ssm_b_im[l, direction].astype(f32),
                ssm_c_re[l, direction].astype(f32), ssm_c_im[l, direction].astype(f32),
                rev))
        y_ssm = (y_dirs[0] + y_dirs[1]).swapaxes(0, 1).reshape(b, s, SSM_WIDTH)
        y_ssm = (y_ssm + ssm_d[l].astype(f32) * u32).astype(x.dtype)
        glu = jax.nn.gelu(y_ssm) @ w_glu[l] + b_glu[l]
        ssm_out = glu[..., :SSM_WIDTH] * jax.nn.sigmoid(glu[..., SSM_WIDTH:])

        merged = (jax.nn.sigmoid(g_attn) * (attn @ w_o_attn[l])
                  + jax.nn.sigmoid(g_ssm) * (ssm_out @ w_o_ssm[l]))
        h = h + merged @ w_out[l]

        h = h + FFN_RES * _swiglu(_rmsnorm(h, ffn2_norm[l]), ffn2_w_gate[l], ffn2_w_up[l], ffn2_w_down[l])
    return _rmsnorm(h, final_norm)
```

```python
import functools
import math

import jax
import jax.numpy as jnp
from jax import lax
from jax.experimental import pallas as pl
from jax.experimental.pallas import tpu as pltpu

F32 = jnp.float32
BF16 = jnp.bfloat16

MLA_HEADS = 8
QK_NOPE_DIM = 64
QK_ROPE_DIM = 32
V_HEAD_DIM = 64
ROPE_THETA = 10000.0
FFN_RES = 0.5
EPS = 1e-6

LANES = 128
SUBLANES = 8
MXU_DIM = 256
VMEM_LIMIT_BYTES = 56 * 1024 * 1024

HEAD_PAD = LANES
SSM_CHUNK = 16

TOKEN_TILE = 512
ATTN_TQ = 256
ATTN_TK = 512
FF_CHUNK = 256


def _cparams(sem):
    return pltpu.CompilerParams(dimension_semantics=sem, vmem_limit_bytes=VMEM_LIMIT_BYTES)


def _rms(x, g):
    return x * lax.rsqrt(jnp.mean(x * x, axis=-1, keepdims=True) + EPS) * g


def _dot(a, b):
    return jnp.dot(a, b, preferred_element_type=F32)


def _dot_nt(a, b):
    return lax.dot_general(a, b, (((1,), (1,)), ((), ())), preferred_element_type=F32)


def _dot_exact(a, b):
    return jnp.dot(a, b, preferred_element_type=F32, precision=lax.Precision.HIGHEST)


def _rope_body(pos_ref, invf_ref, cos_ref, sin_ref):
    ang = pos_ref[...] * invf_ref[...]
    cos_ref[...] = jnp.cos(ang)
    sin_ref[...] = jnp.sin(ang)


def _rope_tables(positions):
    b, s = positions.shape
    half = QK_ROPE_DIM // 2
    per_row = LANES // half
    inv_freq = ROPE_THETA ** (-jnp.arange(0, QK_ROPE_DIM, 2, dtype=F32) / QK_ROPE_DIM)
    pos = jnp.repeat(positions.astype(F32), half, axis=-1).reshape(b, s // per_row, LANES)
    invf = jnp.tile(inv_freq, per_row).reshape(1, LANES)
    rows = s // per_row
    spec = pl.BlockSpec((1, rows, LANES), lambda i: (i, 0, 0))
    cos, sin = pl.pallas_call(
        _rope_body,
        grid=(b,),
        in_specs=[spec, pl.BlockSpec((1, LANES), lambda i: (0, 0))],
        out_specs=[spec, spec],
        out_shape=[jax.ShapeDtypeStruct((b, rows, LANES), F32)] * 2,
        compiler_params=_cparams(("parallel",)),
        name="rope_tables",
    )(pos, invf)
    return cos.reshape(b, s, half), sin.reshape(b, s, half)


def _ffn_body(x_ref, g_ref, wg_ref, wu_ref, wd_ref, *rest, n_chunks, final):
    o_ref = rest[-1]
    x = x_ref[...]
    n = _rms(x, g_ref[...]).astype(BF16)
    acc = jnp.zeros_like(x)
    for j in range(n_chunks):
        gate = _dot(n, wg_ref[j])
        up = _dot(n, wu_ref[j])
        act = (gate * jax.nn.sigmoid(gate) * up).astype(BF16)
        acc = acc + _dot(act, wd_ref[j])
    h = x + FFN_RES * acc
    if final:
        h = _rms(h, rest[0][...])
    o_ref[...] = h


def _ffn(x2, norm, w_gate, w_up, w_down, final_norm=None):
    t, d = x2.shape
    d_ff = w_gate.shape[1]
    n_chunks = d_ff // FF_CHUNK
    wg = w_gate.astype(BF16).reshape(d, n_chunks, FF_CHUNK).transpose(1, 0, 2)
    wu = w_up.astype(BF16).reshape(d, n_chunks, FF_CHUNK).transpose(1, 0, 2)
    wd = w_down.astype(BF16).reshape(n_chunks, FF_CHUNK, d)
    tm = min(TOKEN_TILE, t)
    row = pl.BlockSpec((tm, d), lambda i: (i, 0))
    vec = pl.BlockSpec((1, d), lambda i: (0, 0))
    wspec_in = pl.BlockSpec((n_chunks, d, FF_CHUNK), lambda i: (0, 0, 0))
    wspec_out = pl.BlockSpec((n_chunks, FF_CHUNK, d), lambda i: (0, 0, 0))
    args = [x2, norm.reshape(1, d).astype(F32), wg, wu, wd]
    in_specs = [row, vec, wspec_in, wspec_in, wspec_out]
    if final_norm is not None:
        args.append(final_norm.reshape(1, d).astype(F32))
        in_specs.append(vec)
    return pl.pallas_call(
        functools.partial(_ffn_body, n_chunks=n_chunks, final=final_norm is not None),
        grid=(t // tm,),
        in_specs=in_specs,
        out_specs=row,
        out_shape=jax.ShapeDtypeStruct((t, d), F32),
        compiler_params=_cparams(("parallel",)),
        name="ffn_final" if final_norm is not None else "ffn",
    )(*args)


def _inproj_body(h_ref, g_ref, win_ref, qn_ref, kvn_ref, wq_ref, wqr_ref, wk_ref, wvt_ref,
                 cos_ref, sin_ref, place_ref, base_ref,
                 q_ref, k_ref, vt_ref, u_ref, ga_ref, gs_ref, *, dims):
    q_rank, kv_rank, ssm_w, d_model, scale = dims
    n = _rms(h_ref[0], g_ref[...]).astype(BF16)
    z = _dot(n, win_ref[...])
    o = 0
    c_q = z[:, o:o + q_rank]; o += q_rank
    c_kv = z[:, o:o + kv_rank]; o += kv_rank
    kr = z[:, o:o + HEAD_PAD]; o += HEAD_PAD
    kr_rot = z[:, o:o + HEAD_PAD]; o += HEAD_PAD
    u = z[:, o:o + ssm_w]; o += ssm_w
    g_attn = z[:, o:o + d_model]; o += d_model
    g_ssm = z[:, o:o + d_model]

    u_ref[0] = u.astype(BF16)
    ga_ref[0] = jax.nn.sigmoid(g_attn).astype(BF16)
    gs_ref[0] = jax.nn.sigmoid(g_ssm).astype(BF16)

    cos = _dot_exact(cos_ref[0], place_ref[...]) + base_ref[...]
    sin = _dot_exact(sin_ref[0], place_ref[...])

    cqn = _rms(c_q, qn_ref[...]).astype(BF16)
    q_all = _dot(cqn, wq_ref[...])
    q_rot = _dot(cqn, wqr_ref[...])
    ckvn = _rms(c_kv, kvn_ref[...]).astype(BF16)
    k_nope = _dot(ckvn, wk_ref[...])
    k_rope = kr * cos + kr_rot * sin
    for h in range(MLA_HEADS):
        sl = slice(h * HEAD_PAD, (h + 1) * HEAD_PAD)
        q_ref[0, h] = ((q_all[:, sl] * cos + q_rot[:, sl] * sin) * scale).astype(BF16)
        k_ref[0, h] = (k_nope[:, sl] + k_rope).astype(BF16)
    vt_ref[0, 0] = _dot_nt(wvt_ref[...], ckvn).astype(BF16)


def _place_cols(w, starts, width, total):
    out = jnp.zeros((w.shape[0], total), w.dtype)
    for i, s in enumerate(starts):
        out = lax.dynamic_update_slice(out, w[:, i * width:(i + 1) * width], (0, s))
    return out


def _rot_half(w):
    half = w.shape[1] // 2
    return jnp.concatenate([-w[:, half:], w[:, :half]], axis=1)


def _inproj(h1, cos_c, sin_c, mix_norm, w_in, q_norm, w_uq, kv_norm, w_ukv, ssm_w):
    b, s, d = h1.shape
    q_rank, kv_rank = q_norm.shape[0], kv_norm.shape[0]
    hq = QK_NOPE_DIM + QK_ROPE_DIM
    hkv = QK_NOPE_DIM + V_HEAD_DIM
    scale = float(hq) ** -0.5
    o_kr = q_rank + kv_rank
    o_u = o_kr + QK_ROPE_DIM
    w_kr = w_in[:, o_kr:o_u]
    kr_p = _place_cols(w_kr, [QK_NOPE_DIM], QK_ROPE_DIM, HEAD_PAD)
    kr_rot_p = _place_cols(_rot_half(w_kr), [QK_NOPE_DIM], QK_ROPE_DIM, HEAD_PAD)
    win = jnp.concatenate([w_in[:, :o_kr], kr_p, kr_rot_p, w_in[:, o_u:]], axis=1).astype(BF16)
    wq = jnp.concatenate([jnp.pad(w_uq[:, h * hq:(h + 1) * hq], ((0, 0), (0, HEAD_PAD - hq)))
                          for h in range(MLA_HEADS)], axis=1).astype(BF16)
    wqr = jnp.concatenate(
        [_place_cols(_rot_half(w_uq[:, h * hq + QK_NOPE_DIM:(h + 1) * hq]), [QK_NOPE_DIM], QK_ROPE_DIM, HEAD_PAD)
         for h in range(MLA_HEADS)], axis=1).astype(BF16)
    wk = jnp.concatenate([jnp.pad(w_ukv[:, h * hkv:h * hkv + QK_NOPE_DIM], ((0, 0), (0, HEAD_PAD - QK_NOPE_DIM)))
                          for h in range(MLA_HEADS)], axis=1).astype(BF16)
    wvt = jnp.concatenate([w_ukv[:, h * hkv + QK_NOPE_DIM:(h + 1) * hkv] for h in range(MLA_HEADS)],
                          axis=1).T.astype(BF16)
    half = QK_ROPE_DIM // 2
    eye = jnp.eye(half, dtype=F32)
    place = _place_cols(jnp.concatenate([eye, eye], axis=1), [QK_NOPE_DIM], QK_ROPE_DIM, HEAD_PAD)
    base = (jnp.arange(HEAD_PAD) < QK_NOPE_DIM).astype(F32).reshape(1, HEAD_PAD)

    tm = min(TOKEN_TILE, s)
    nt = s // tm
    hv = MLA_HEADS * V_HEAD_DIM
    const = lambda shape: pl.BlockSpec(shape, lambda i, j: (0,) * len(shape))
    tok = lambda w: pl.BlockSpec((1, tm, w), lambda i, j: (i, j, 0))
    heads = pl.BlockSpec((1, MLA_HEADS, tm, HEAD_PAD), lambda i, j: (i, 0, j, 0))
    dims = (q_rank, kv_rank, ssm_w, d, scale)
    return pl.pallas_call(
        functools.partial(_inproj_body, dims=dims),
        grid=(b, nt),
        in_specs=[tok(d), const((1, d)), const(win.shape), const((1, q_rank)), const((1, kv_rank)),
                  const(wq.shape), const(wqr.shape), const(wk.shape), const(wvt.shape),
                  tok(half), tok(half), const(place.shape), const(base.shape)],
        out_specs=[heads, heads, pl.BlockSpec((1, 1, hv, tm), lambda i, j: (i, j, 0, 0)),
                   tok(ssm_w), tok(d), tok(d)],
        out_shape=[jax.ShapeDtypeStruct((b, MLA_HEADS, s, HEAD_PAD), BF16),
                   jax.ShapeDtypeStruct((b, MLA_HEADS, s, HEAD_PAD), BF16),
                   jax.ShapeDtypeStruct((b, nt, hv, tm), BF16),
                   jax.ShapeDtypeStruct((b, s, ssm_w), BF16),
                   jax.ShapeDtypeStruct((b, s, d), BF16),
                   jax.ShapeDtypeStruct((b, s, d), BF16)],
        compiler_params=_cparams(("parallel", "parallel")),
        name="inproj",
    )(h1, mix_norm.reshape(1, d).astype(F32), win, q_norm.reshape(1, q_rank).astype(F32),
      kv_norm.reshape(1, kv_rank).astype(F32), wq, wqr, wk, wvt, cos_c, sin_c, place, base)


def _attn_body(q_ref, k_ref, vt_ref, o_ref, *, n_kt, tk):
    q = q_ref[0, 0]
    tq = q.shape[0]
    m = jnp.full((1, tq), -jnp.inf, F32)
    l = jnp.zeros((1, tq), F32)
    acc = jnp.zeros((V_HEAD_DIM, tq), F32)
    for j in range(n_kt):
        s = _dot_nt(k_ref[0, 0, j * tk:(j + 1) * tk, :], q)
        m_new = jnp.maximum(m, jnp.max(s, axis=0, keepdims=True))
        alpha = jnp.exp(m - m_new)
        p = jnp.exp(s - m_new)
        l = alpha * l + jnp.sum(p, axis=0, keepdims=True)
        acc = alpha * acc + _dot(vt_ref[0, j], p.astype(BF16))
        m = m_new
    o_ref[0] = (acc / l).astype(BF16)


def _attention(q, k, vt):
    b, h, s, _ = q.shape
    n_kt, tk = vt.shape[1], vt.shape[3]
    tq = min(ATTN_TQ, s)
    return pl.pallas_call(
        functools.partial(_attn_body, n_kt=n_kt, tk=tk),
        grid=(b, h, s // tq),
        in_specs=[pl.BlockSpec((1, 1, tq, HEAD_PAD), lambda i, j, t: (i, j, t, 0)),
                  pl.BlockSpec((1, 1, s, HEAD_PAD), lambda i, j, t: (i, j, 0, 0)),
                  pl.BlockSpec((1, n_kt, V_HEAD_DIM, tk), lambda i, j, t: (i, 0, j, 0))],
        out_specs=pl.BlockSpec((1, V_HEAD_DIM, tq), lambda i, j, t: (i, j, t)),
        out_shape=jax.ShapeDtypeStruct((b, h * V_HEAD_DIM, s), BF16),
        compiler_params=_cparams(("parallel", "parallel", "arbitrary")),
        name="attention",
    )(q, k, vt)


def _ssm_prep_body(rowp_ref, colp_ref, bt_re_ref, bt_im_ref, ct_re_ref, ct_im_ref, dt_ref,
                   tt_ref, pp_ref, qq_ref, dec_ref, *, n_state):
    L = SSM_CHUNK
    rows = tt_ref.shape[1]
    width = rows // L
    lanes2 = 2 * n_state
    parity = pl.program_id(0) % 2
    lane = lax.broadcasted_iota(jnp.int32, (1, lanes2), 1)
    keep_lane = (lane >= parity * n_state) & (lane < (parity + 1) * n_state)
    first_lane = lane < n_state
    sub = lax.broadcasted_iota(jnp.int32, (lanes2, 1), 0)
    keep_sub = (sub >= parity * n_state) & (sub < (parity + 1) * n_state)
    krow = (lax.broadcasted_iota(jnp.int32, (rows, 1), 0) // width).astype(F32)
    tl = lax.broadcasted_iota(jnp.int32, (1, rows), 1) // width

    k_tl = []
    for d in range(2):
        lr, li, ldt = rowp_ref[0, d, 0:1, :], rowp_ref[0, d, 1:2, :], rowp_ref[0, d, 2:3, :]
        dt = jnp.exp(ldt)
        mag = jnp.exp(lr * dt)
        lb_re = mag * jnp.cos(li * dt)
        lb_im = mag * jnp.sin(li * dt)
        den = lr * lr + li * li
        nr = lb_re - 1.0
        k_re = (nr * lr + lb_im * li) / den
        k_im = (lb_im * lr - nr * li) / den
        bt_re, bt_im = bt_re_ref[0, d], bt_im_ref[0, d]
        bb_re = k_re * bt_re - k_im * bt_im
        bb_im = k_re * bt_im + k_im * bt_re
        magk = jnp.exp(krow * (lr * dt))
        ak_re = magk * jnp.cos(krow * (li * dt))
        ak_im = magk * jnp.sin(krow * (li * dt))
        ba_re = bb_re * ak_re - bb_im * ak_im
        ba_im = bb_re * ak_im + bb_im * ak_re
        ct_re, ct_im = ct_re_ref[0, d], ct_im_ref[0, d]
        k_tl.append(_dot_exact(jnp.where(first_lane, ba_re, 0.0), ct_re)
                    - _dot_exact(jnp.where(first_lane, ba_im, 0.0), ct_im))
        blocks = range(L - 1, -1, -1) if d == 0 else range(L)
        p_re = jnp.concatenate([ba_re[k * width:(k + 1) * width] for k in blocks], axis=0)
        p_im = jnp.concatenate([ba_im[k * width:(k + 1) * width] for k in blocks], axis=0)
        pp_ref[0, :, (2 * d) * lanes2:(2 * d + 1) * lanes2] = jnp.where(keep_lane, p_re, 0.0).astype(BF16)
        pp_ref[0, :, (2 * d + 1) * lanes2:(2 * d + 2) * lanes2] = jnp.where(keep_lane, p_im, 0.0).astype(BF16)
        lrc, lic, ldtc = colp_ref[0, d, :, 0:1], colp_ref[0, d, :, 1:2], colp_ref[0, d, :, 2:3]
        dtc = jnp.exp(ldtc)
        kq = (tl + 1).astype(F32) if d == 0 else (L - tl).astype(F32)
        magq = jnp.exp((lrc * dtc) * kq)
        aq_re = magq * jnp.cos((lic * dtc) * kq)
        aq_im = magq * jnp.sin((lic * dtc) * kq)
        q_re = ct_re * aq_re - ct_im * aq_im
        q_im = -(ct_re * aq_im + ct_im * aq_re)
        qq_ref[0, (2 * d) * lanes2:(2 * d + 1) * lanes2, :] = jnp.where(keep_sub, q_re, 0.0).astype(BF16)
        qq_ref[0, (2 * d + 1) * lanes2:(2 * d + 2) * lanes2, :] = jnp.where(keep_sub, q_im, 0.0).astype(BF16)
        magl = jnp.exp(float(L) * (lr * dt))
        dec_ref[0, 2 * d:2 * d + 1, :] = jnp.where(keep_lane, magl * jnp.cos(float(L) * (li * dt)), 0.0)
        dec_ref[0, 2 * d + 1:2 * d + 2, :] = jnp.where(keep_lane, magl * jnp.sin(float(L) * (li * dt)), 0.0)
    dec_ref[0, 4:8, :] = jnp.zeros((4, lanes2), F32)

    kf, kb = k_tl
    blk = lambda a, k: a[k * width:(k + 1) * width]
    diag = blk(kf, 0) + blk(kb, 0) + dt_ref[0]
    for s in range(L):
        acc = jnp.where(tl == s, diag, 0.0)
        for k in range(1, L - s):
            acc = jnp.where(tl == s + k, blk(kf, k), acc)
        for k in range(1, s + 1):
            acc = jnp.where(tl == s - k, blk(kb, k), acc)
        tt_ref[0, s * width:(s + 1) * width, :] = acc.astype(BF16)


def _ssm_prep(lam_re, lam_im, log_dt, b_re, b_im, c_re, c_im, d_skip):
    _, g, p, w = b_re.shape
    L = SSM_CHUNK
    rows = L * w
    dup = lambda a: jnp.concatenate([a, a], axis=-1)
    ldt = jnp.broadcast_to(log_dt[:, :, None], lam_re.shape)
    rowp = jnp.stack([dup(lam_re), dup(lam_im), dup(ldt)], axis=2)
    rowp = jnp.pad(rowp, ((0, 0), (0, 0), (0, SUBLANES - 3), (0, 0))).transpose(1, 0, 2, 3)
    colp = jnp.stack([dup(lam_re), dup(lam_im), dup(ldt)], axis=3)
    colp = jnp.pad(colp, ((0, 0), (0, 0), (0, 0), (0, LANES - 3))).transpose(1, 0, 2, 3)
    bt = lambda b: jnp.tile(dup(jnp.swapaxes(b, 2, 3)), (1, 1, L, 1)).transpose(1, 0, 2, 3)
    ct = lambda c: jnp.tile(jnp.concatenate([jnp.swapaxes(c, 2, 3)] * 2, axis=2),
                            (1, 1, 1, L)).transpose(1, 0, 2, 3)
    dmat = jnp.where(jnp.eye(w, dtype=bool)[None], d_skip.reshape(g, 1, w), 0.0)
    dtile = jnp.tile(dmat, (1, 1, L)).astype(F32)
    grp = lambda shape: pl.BlockSpec((1,) + shape, lambda i: (i,) + (0,) * len(shape))
    return pl.pallas_call(
        functools.partial(_ssm_prep_body, n_state=p),
        grid=(g,),
        in_specs=[grp((2, SUBLANES, 2 * p)), grp((2, 2 * p, LANES)), grp((2, rows, 2 * p)), grp((2, rows, 2 * p)),
                  grp((2, 2 * p, rows)), grp((2, 2 * p, rows)), grp((w, rows))],
        out_specs=[grp((rows, rows)), grp((rows, 8 * p)), grp((8 * p, rows)), grp((SUBLANES, 2 * p))],
        out_shape=[jax.ShapeDtypeStruct((g, rows, rows), BF16),
                   jax.ShapeDtypeStruct((g, rows, 8 * p), BF16),
                   jax.ShapeDtypeStruct((g, 8 * p, rows), BF16),
                   jax.ShapeDtypeStruct((g, SUBLANES, 2 * p), F32)],
        compiler_params=_cparams(("parallel",)),
        name="ssm_prep",
    )(rowp.astype(F32), colp.astype(F32), bt(b_re).astype(F32), bt(b_im).astype(F32),
      ct(c_re).astype(F32), ct(c_im).astype(F32), dtile)


def _ssm_body(u_ref, tt_ref, pp_ref, qq_ref, dec_ref, y_ref, st_ref, *, n_chunks):
    bsz = SUBLANES
    w = LANES
    st_ref[...] = _dot(u_ref[0], pp_ref[0]) + _dot(u_ref[1], pp_ref[1])
    dec = dec_ref[0] + dec_ref[1]
    a_fr, a_fi, a_br, a_bi = (jnp.broadcast_to(dec[i:i + 1, :], (bsz, w)) for i in range(4))

    def step(c, carry):
        hfr, hfi, hbr, hbi = carry
        rf = pl.multiple_of(c * bsz, bsz)
        rb = pl.multiple_of((n_chunks - 1 - c) * bsz, bsz)
        efr = st_ref[pl.ds(rf, bsz), 0 * w:1 * w]
        efi = st_ref[pl.ds(rf, bsz), 1 * w:2 * w]
        ebr = st_ref[pl.ds(rb, bsz), 2 * w:3 * w]
        ebi = st_ref[pl.ds(rb, bsz), 3 * w:4 * w]
        st_ref[pl.ds(rf, bsz), 0 * w:1 * w] = hfr
        st_ref[pl.ds(rf, bsz), 1 * w:2 * w] = hfi
        st_ref[pl.ds(rb, bsz), 2 * w:3 * w] = hbr
        st_ref[pl.ds(rb, bsz), 3 * w:4 * w] = hbi
        return (a_fr * hfr - a_fi * hfi + efr, a_fi * hfr + a_fr * hfi + efi,
                a_br * hbr - a_bi * hbi + ebr, a_bi * hbr + a_br * hbi + ebi)

    zero = jnp.zeros((bsz, w), F32)
    lax.fori_loop(0, n_chunks, step, (zero, zero, zero, zero))
    st = st_ref[...].astype(BF16)
    for g in range(2):
        y_ref[g] = (_dot(u_ref[g], tt_ref[g]) + _dot(st, qq_ref[g])).astype(BF16)


def _ssm(u_rows, tt, pp, qq, dec):
    g, r, k = u_rows.shape
    n_state8 = pp.shape[2]
    pair = lambda shape: pl.BlockSpec((2,) + shape, lambda i: (i,) + (0,) * len(shape))
    return pl.pallas_call(
        functools.partial(_ssm_body, n_chunks=r // SUBLANES),
        grid=(g // 2,),
        in_specs=[pair((r, k)), pair((k, k)), pair((k, n_state8)), pair((n_state8, k)),
                  pair((SUBLANES, dec.shape[2]))],
        out_specs=pair((r, k)),
        out_shape=jax.ShapeDtypeStruct((g, r, k), BF16),
        scratch_shapes=[pltpu.VMEM((r, n_state8), F32)],
        compiler_params=_cparams(("parallel",)),
        name="ssm",
    )(u_rows, tt, pp, qq, dec)


def _merge_body(h_ref, at_ref, y_ref, ga_ref, gs_ref, woa_ref, wglu_ref, bglu_ref, wos_ref, wout_ref, o_ref):
    ssm_w = y_ref.shape[2]
    attn = lax.dot_general(at_ref[0], woa_ref[...], (((0,), (0,)), ((), ())), preferred_element_type=F32)
    y = y_ref[0].astype(F32)
    glu = _dot(jax.nn.gelu(y).astype(BF16), wglu_ref[...]) + bglu_ref[...]
    ssm_out = glu[:, :ssm_w] * jax.nn.sigmoid(glu[:, ssm_w:])
    ssm = _dot(ssm_out.astype(BF16), wos_ref[...])
    merged = ga_ref[0].astype(F32) * attn + gs_ref[0].astype(F32) * ssm
    o_ref[0] = h_ref[0] + _dot(merged.astype(BF16), wout_ref[...])


def _merge(h1, attn_t, y_ssm, ga, gs, w_o_attn, w_glu, b_glu, w_o_ssm, w_out):
    b, s, d = h1.shape
    hv = attn_t.shape[1]
    ssm_w = y_ssm.shape[2]
    tm = min(TOKEN_TILE, s)
    const = lambda shape: pl.BlockSpec(shape, lambda i, j: (0,) * len(shape))
    tok = lambda w: pl.BlockSpec((1, tm, w), lambda i, j: (i, j, 0))
    return pl.pallas_call(
        _merge_body,
        grid=(b, s // tm),
        in_specs=[tok(d), pl.BlockSpec((1, hv, tm), lambda i, j: (i, 0, j)), tok(ssm_w), tok(d), tok(d),
                  const(w_o_attn.shape), const(w_glu.shape), const((1, w_glu.shape[1])),
                  const(w_o_ssm.shape), const(w_out.shape)],
        out_specs=tok(d),
        out_shape=jax.ShapeDtypeStruct((b, s, d), F32),
        compiler_params=_cparams(("parallel", "parallel")),
        name="merge",
    )(h1, attn_t, y_ssm, ga, gs, w_o_attn.astype(BF16), w_glu.astype(BF16),
      b_glu.reshape(1, -1).astype(F32), w_o_ssm.astype(BF16), w_out.astype(BF16))


def kernel(x, positions, ffn1_norm, ffn1_w_gate, ffn1_w_up, ffn1_w_down, mix_norm, w_in, q_norm, w_uq, kv_norm, w_ukv, w_o_attn, ssm_lambda_re, ssm_lambda_im, ssm_log_dt, ssm_b_re, ssm_b_im, ssm_c_re, ssm_c_im, ssm_d, w_glu, b_glu, w_o_ssm, w_out, ffn2_norm, ffn2_w_gate, ffn2_w_up, ffn2_w_down, final_norm):
    b, s, d = x.shape
    depth = ffn1_norm.shape[0]
    _, _, n_grp, _, grp_w = ssm_b_re.shape
    ssm_w = n_grp * grp_w
    L = SSM_CHUNK
    assert b == SUBLANES, "the S5 chunk recurrence keeps the batch on the 8 sublanes"
    assert s % max(TOKEN_TILE, ATTN_TQ) == 0 and L * grp_w == MXU_DIM and n_grp % 2 == 0
    n_chunks = s // L

    cos_c, sin_c = _rope_tables(positions)
    h = x
    for l in range(depth):
        last = l == depth - 1
        h = _ffn(h.reshape(b * s, d), ffn1_norm[l], ffn1_w_gate[l], ffn1_w_up[l], ffn1_w_down[l]).reshape(b, s, d)
        q, k, vt, u, ga, gs = _inproj(h, cos_c, sin_c, mix_norm[l], w_in[l], q_norm[l], w_uq[l],
                                      kv_norm[l], w_ukv[l], ssm_w)
        attn_t = _attention(q, k, vt)

        tt, pp, qq, dec = _ssm_prep(ssm_lambda_re[l], ssm_lambda_im[l], ssm_log_dt[l], ssm_b_re[l], ssm_b_im[l],
                                    ssm_c_re[l], ssm_c_im[l], ssm_d[l])
        u_rows = u.reshape(b, n_chunks, L, n_grp, grp_w).transpose(3, 1, 0, 2, 4).reshape(n_grp, n_chunks * b, L * grp_w)
        y_rows = _ssm(u_rows, tt, pp, qq, dec)
        y_ssm = y_rows.reshape(n_grp, n_chunks, b, L, grp_w).transpose(2, 1, 3, 0, 4).reshape(b, s, ssm_w)

        h = _merge(h, attn_t, y_ssm, ga, gs, w_o_attn[l], w_glu[l], b_glu[l], w_o_ssm[l], w_out[l])
        h = _ffn(h.reshape(b * s, d), ffn2_norm[l], ffn2_w_gate[l], ffn2_w_up[l], ffn2_w_down[l],
                 final_norm if last else None).reshape(b, s, d)
    if depth == 0:
        h = _rms(h, final_norm)
    return h
```

```python
import functools
import math

import jax
import jax.numpy as jnp
from jax import lax
from jax.experimental import pallas as pl
from jax.experimental.pallas import tpu as pltpu

F32 = jnp.float32
BF16 = jnp.bfloat16

MLA_HEADS = 8
QK_NOPE_DIM = 64
QK_ROPE_DIM = 32
V_HEAD_DIM = 64
ROPE_THETA = 10000.0
FFN_RES = 0.5
EPS = 1e-6

LANES = 128
SUBLANES = 8
MXU_DIM = 256
VMEM_LIMIT_BYTES = 56 * 1024 * 1024

HEAD_PAD = LANES
V_ONES_ROWS = 16
SSM_CHUNK = 16

TOKEN_TILE = 512
ATTN_TQ = 1024
ATTN_TK = 512
FF_CHUNK = 256

def _cparams(sem, flags=None):
    return pltpu.CompilerParams(dimension_semantics=sem, vmem_limit_bytes=VMEM_LIMIT_BYTES, flags=flags)


def _rms(x, g):
    return x * lax.rsqrt(jnp.mean(x * x, axis=-1, keepdims=True) + EPS) * g


def _dot(a, b):
    return jnp.dot(a, b, preferred_element_type=F32)


def _dot_nt(a, b):
    return lax.dot_general(a, b, (((1,), (1,)), ((), ())), preferred_element_type=F32)


def _dot_exact(a, b):
    return jnp.dot(a, b, preferred_element_type=F32, precision=lax.Precision.HIGHEST)


def _rope_body(pos_ref, invf_ref, cos_ref, sin_ref):
    ang = pos_ref[...] * invf_ref[...]
    cos_ref[...] = jnp.cos(ang)
    sin_ref[...] = jnp.sin(ang)


def _rope_tables(positions):
    b, s = positions.shape
    half = QK_ROPE_DIM // 2
    per_row = LANES // half
    inv_freq = ROPE_THETA ** (-jnp.arange(0, QK_ROPE_DIM, 2, dtype=F32) / QK_ROPE_DIM)
    pos = jnp.repeat(positions.astype(F32), half, axis=-1).reshape(b, s // per_row, LANES)
    invf = jnp.tile(inv_freq, per_row).reshape(1, LANES)
    rows = s // per_row
    spec = pl.BlockSpec((1, rows, LANES), lambda i: (i, 0, 0))
    cos, sin = pl.pallas_call(
        _rope_body,
        grid=(b,),
        in_specs=[spec, pl.BlockSpec((1, LANES), lambda i: (0, 0))],
        out_specs=[spec, spec],
        out_shape=[jax.ShapeDtypeStruct((b, rows, LANES), F32)] * 2,
        compiler_params=_cparams(("parallel",)),
        name="rope_tables",
    )(pos, invf)
    return cos.reshape(b, s, half), sin.reshape(b, s, half)


def _ffn_body(x_ref, g_ref, wg_ref, wu_ref, wd_ref, *rest, n_chunks, final):
    o_ref = rest[-1]
    x = x_ref[...]
    n = _rms(x, g_ref[...]).astype(BF16)
    acc = jnp.zeros_like(x)
    for j in range(n_chunks):
        gate = _dot(n, wg_ref[j])
        up = _dot(n, wu_ref[j])
        act = (gate * jax.nn.sigmoid(gate) * up).astype(BF16)
        acc = acc + _dot(act, wd_ref[j])
    h = x + FFN_RES * acc
    if final:
        h = _rms(h, rest[0][...])
    o_ref[...] = h


def _ffn(x2, norm, w_gate, w_up, w_down, final_norm=None):
    t, d = x2.shape
    d_ff = w_gate.shape[1]
    n_chunks = d_ff // FF_CHUNK
    wg = w_gate.astype(BF16).reshape(d, n_chunks, FF_CHUNK).transpose(1, 0, 2)
    wu = w_up.astype(BF16).reshape(d, n_chunks, FF_CHUNK).transpose(1, 0, 2)
    wd = w_down.astype(BF16).reshape(n_chunks, FF_CHUNK, d)
    tm = min(TOKEN_TILE, t)
    row = pl.BlockSpec((tm, d), lambda i: (i, 0))
    vec = pl.BlockSpec((1, d), lambda i: (0, 0))
    wspec_in = pl.BlockSpec((n_chunks, d, FF_CHUNK), lambda i: (0, 0, 0))
    wspec_out = pl.BlockSpec((n_chunks, FF_CHUNK, d), lambda i: (0, 0, 0))
    args = [x2, norm.reshape(1, d).astype(F32), wg, wu, wd]
    in_specs = [row, vec, wspec_in, wspec_in, wspec_out]
    if final_norm is not None:
        args.append(final_norm.reshape(1, d).astype(F32))
        in_specs.append(vec)
    return pl.pallas_call(
        functools.partial(_ffn_body, n_chunks=n_chunks, final=final_norm is not None),
        grid=(t // tm,),
        in_specs=in_specs,
        out_specs=row,
        out_shape=jax.ShapeDtypeStruct((t, d), F32),
        compiler_params=_cparams(("parallel",)),
        name="ffn_final" if final_norm is not None else "ffn",
    )(*args)


def _inproj_body(h_ref, g_ref, win_ref, qn_ref, kvn_ref, wq_ref, wqr_ref, wk_ref, wvt_ref,
                 cos_ref, sin_ref, place_ref, base_ref,
                 q_ref, k_ref, vt_ref, u_ref, ga_ref, gs_ref, *, dims):
    q_rank, kv_rank, ssm_w, d_model, scale = dims
    n = _rms(h_ref[0], g_ref[...]).astype(BF16)
    z = _dot(n, win_ref[...])
    o = 0
    c_q = z[:, o:o + q_rank]; o += q_rank
    c_kv = z[:, o:o + kv_rank]; o += kv_rank
    kr = z[:, o:o + HEAD_PAD]; o += HEAD_PAD
    kr_rot = z[:, o:o + HEAD_PAD]; o += HEAD_PAD
    u = z[:, o:o + ssm_w]; o += ssm_w
    g_attn = z[:, o:o + d_model]; o += d_model
    g_ssm = z[:, o:o + d_model]

    u_ref[0] = u.astype(BF16)
    ga_ref[0] = jax.nn.sigmoid(g_attn).astype(BF16)
    gs_ref[0] = jax.nn.sigmoid(g_ssm).astype(BF16)

    cos = _dot_exact(cos_ref[0], place_ref[...]) + base_ref[...]
    sin = _dot_exact(sin_ref[0], place_ref[...])

    cqn = _rms(c_q, qn_ref[...]).astype(BF16)
    q_all = _dot(cqn, wq_ref[...])
    q_rot = _dot(cqn, wqr_ref[...])
    ckvn = _rms(c_kv, kvn_ref[...]).astype(BF16)
    k_nope = _dot(ckvn, wk_ref[...])
    k_rope = kr * cos + kr_rot * sin
    for h in range(MLA_HEADS):
        sl = slice(h * HEAD_PAD, (h + 1) * HEAD_PAD)
        q_ref[0, h] = ((q_all[:, sl] * cos + q_rot[:, sl] * sin) * scale).astype(BF16)
        k_ref[0, h] = (k_nope[:, sl] + k_rope).astype(BF16)
    vt = _dot_nt(wvt_ref[...], ckvn).astype(BF16)
    ones = jnp.ones((V_ONES_ROWS, vt.shape[1]), BF16)
    vt_ref[0, 0] = jnp.concatenate(
        [piece for h in range(MLA_HEADS) for piece in (vt[h * V_HEAD_DIM:(h + 1) * V_HEAD_DIM], ones)], axis=0)


def _place_cols(w, starts, width, total):
    out = jnp.zeros((w.shape[0], total), w.dtype)
    for i, s in enumerate(starts):
        out = lax.dynamic_update_slice(out, w[:, i * width:(i + 1) * width], (0, s))
    return out


def _rot_half(w):
    half = w.shape[1] // 2
    return jnp.concatenate([-w[:, half:], w[:, :half]], axis=1)


def _inproj(h1, cos_c, sin_c, mix_norm, w_in, q_norm, w_uq, kv_norm, w_ukv, ssm_w):
    b, s, d = h1.shape
    q_rank, kv_rank = q_norm.shape[0], kv_norm.shape[0]
    hq = QK_NOPE_DIM + QK_ROPE_DIM
    hkv = QK_NOPE_DIM + V_HEAD_DIM
    scale = float(hq) ** -0.5 * math.log2(math.e)
    o_kr = q_rank + kv_rank
    o_u = o_kr + QK_ROPE_DIM
    w_kr = w_in[:, o_kr:o_u]
    kr_p = _place_cols(w_kr, [QK_NOPE_DIM], QK_ROPE_DIM, HEAD_PAD)
    kr_rot_p = _place_cols(_rot_half(w_kr), [QK_NOPE_DIM], QK_ROPE_DIM, HEAD_PAD)
    win = jnp.concatenate([w_in[:, :o_kr], kr_p, kr_rot_p, w_in[:, o_u:]], axis=1).astype(BF16)
    wq = jnp.concatenate([jnp.pad(w_uq[:, h * hq:(h + 1) * hq], ((0, 0), (0, HEAD_PAD - hq)))
                          for h in range(MLA_HEADS)], axis=1).astype(BF16)
    wqr = jnp.concatenate(
        [_place_cols(_rot_half(w_uq[:, h * hq + QK_NOPE_DIM:(h + 1) * hq]), [QK_NOPE_DIM], QK_ROPE_DIM, HEAD_PAD)
         for h in range(MLA_HEADS)], axis=1).astype(BF16)
    wk = jnp.concatenate([jnp.pad(w_ukv[:, h * hkv:h * hkv + QK_NOPE_DIM], ((0, 0), (0, HEAD_PAD - QK_NOPE_DIM)))
                          for h in range(MLA_HEADS)], axis=1).astype(BF16)
    wvt = jnp.concatenate([w_ukv[:, h * hkv + QK_NOPE_DIM:(h + 1) * hkv] for h in range(MLA_HEADS)],
                          axis=1).T.astype(BF16)
    half = QK_ROPE_DIM // 2
    eye = jnp.eye(half, dtype=F32)
    place = _place_cols(jnp.concatenate([eye, eye], axis=1), [QK_NOPE_DIM], QK_ROPE_DIM, HEAD_PAD)
    base = (jnp.arange(HEAD_PAD) < QK_NOPE_DIM).astype(F32).reshape(1, HEAD_PAD)

    tm = min(TOKEN_TILE, s)
    nt = s // tm
    hv = MLA_HEADS * (V_HEAD_DIM + V_ONES_ROWS)
    const = lambda shape: pl.BlockSpec(shape, lambda i, j: (0,) * len(shape))
    tok = lambda w: pl.BlockSpec((1, tm, w), lambda i, j: (i, j, 0))
    heads = pl.BlockSpec((1, MLA_HEADS, tm, HEAD_PAD), lambda i, j: (i, 0, j, 0))
    dims = (q_rank, kv_rank, ssm_w, d, scale)
    return pl.pallas_call(
        functools.partial(_inproj_body, dims=dims),
        grid=(b, nt),
        in_specs=[tok(d), const((1, d)), const(win.shape), const((1, q_rank)), const((1, kv_rank)),
                  const(wq.shape), const(wqr.shape), const(wk.shape), const(wvt.shape),
                  tok(half), tok(half), const(place.shape), const(base.shape)],
        out_specs=[heads, heads, pl.BlockSpec((1, 1, hv, tm), lambda i, j: (i, j, 0, 0)),
                   tok(ssm_w), tok(d), tok(d)],
        out_shape=[jax.ShapeDtypeStruct((b, MLA_HEADS, s, HEAD_PAD), BF16),
                   jax.ShapeDtypeStruct((b, MLA_HEADS, s, HEAD_PAD), BF16),
                   jax.ShapeDtypeStruct((b, nt, hv, tm), BF16),
                   jax.ShapeDtypeStruct((b, s, ssm_w), BF16),
                   jax.ShapeDtypeStruct((b, s, d), BF16),
                   jax.ShapeDtypeStruct((b, s, d), BF16)],
        compiler_params=_cparams(("parallel", "parallel")),
        name="inproj",
    )(h1, mix_norm.reshape(1, d).astype(F32), win, q_norm.reshape(1, q_rank).astype(F32),
      kv_norm.reshape(1, kv_rank).astype(F32), wq, wqr, wk, wvt, cos_c, sin_c, place, base)


def _attn_body(q_ref, k_ref, vt_ref, o_ref, s0_ref, s1_ref, acc_ref, *, n_kt, tk):
    q = q_ref[0, 0]
    tq = q.shape[0]
    s_ref = (s0_ref, s1_ref)

    def scores(j, slot):
        start = pl.multiple_of(j * tk, tk)
        s_ref[slot][...] = _dot_nt(k_ref[0, 0, pl.ds(start, tk), :], q)

    def consume(j, slot, m):
        s = s_ref[slot][...]
        m_new = jnp.maximum(m, jnp.max(s, axis=0, keepdims=True))
        p = jnp.exp2(s - m_new).astype(BF16)
        acc_ref[...] = jnp.exp2(m - m_new) * acc_ref[...] + _dot(vt_ref[0, j], p)
        return m_new

    def pair(i, m):
        scores(2 * i + 1, 1)
        m = consume(2 * i, 0, m)
        scores(2 * i + 2, 0)
        return consume(2 * i + 1, 1, m)

    assert n_kt == 1 or n_kt % 2 == 0
    acc_ref[...] = jnp.zeros_like(acc_ref)
    m = jnp.full((1, tq), -jnp.inf, F32)
    scores(0, 0)
    if n_kt > 1:
        for i in range(n_kt // 2 - 1):
            m = pair(i, m)
        scores(n_kt - 1, 1)
        m = consume(n_kt - 2, 0, m)
        consume(n_kt - 1, 1, m)
    else:
        consume(0, 0, m)
    acc = acc_ref[...]
    o_ref[0] = (acc[:V_HEAD_DIM] / acc[V_HEAD_DIM:V_HEAD_DIM + 1]).astype(BF16)


def _attention(q, k, vt):
    b, h, s, _ = q.shape
    n_kt, tk = vt.shape[1], vt.shape[3]
    vrows = vt.shape[2] // h
    tq = min(ATTN_TQ, s)
    return pl.pallas_call(
        functools.partial(_attn_body, n_kt=n_kt, tk=tk),
        grid=(b, h, s // tq),
        in_specs=[pl.BlockSpec((1, 1, tq, HEAD_PAD), lambda i, j, t: (i, j, t, 0)),
                  pl.BlockSpec((1, 1, s, HEAD_PAD), lambda i, j, t: (i, j, 0, 0)),
                  pl.BlockSpec((1, n_kt, vrows, tk), lambda i, j, t: (i, 0, j, 0))],
        out_specs=pl.BlockSpec((1, V_HEAD_DIM, tq), lambda i, j, t: (i, j, t)),
        out_shape=jax.ShapeDtypeStruct((b, h * V_HEAD_DIM, s), BF16),
        scratch_shapes=[pltpu.VMEM((tk, tq), F32), pltpu.VMEM((tk, tq), F32), pltpu.VMEM((vrows, tq), F32)],
        compiler_params=_cparams(("parallel", "parallel", "arbitrary")),
        name="attention",
    )(q, k, vt)


def _ssm_prep_body(rowp_ref, colp_ref, bt_re_ref, bt_im_ref, ct_re_ref, ct_im_ref, dt_ref,
                   tt_ref, pp_ref, qq_ref, dec_ref, *, n_state):
    L = SSM_CHUNK
    rows = tt_ref.shape[1]
    width = rows // L
    lanes2 = 2 * n_state
    parity = pl.program_id(0) % 2
    lane = lax.broadcasted_iota(jnp.int32, (1, lanes2), 1)
    keep_lane = (lane >= parity * n_state) & (lane < (parity + 1) * n_state)
    first_lane = lane < n_state
    sub = lax.broadcasted_iota(jnp.int32, (lanes2, 1), 0)
    keep_sub = (sub >= parity * n_state) & (sub < (parity + 1) * n_state)
    krow = (lax.broadcasted_iota(jnp.int32, (rows, 1), 0) // width).astype(F32)
    tl = lax.broadcasted_iota(jnp.int32, (1, rows), 1) // width

    k_tl = []
    for d in range(2):
        lr, li, ldt = rowp_ref[0, d, 0:1, :], rowp_ref[0, d, 1:2, :], rowp_ref[0, d, 2:3, :]
        dt = jnp.exp(ldt)
        mag = jnp.exp(lr * dt)
        lb_re = mag * jnp.cos(li * dt)
        lb_im = mag * jnp.sin(li * dt)
        den = lr * lr + li * li
        nr = lb_re - 1.0
        k_re = (nr * lr + lb_im * li) / den
        k_im = (lb_im * lr - nr * li) / den
        bt_re, bt_im = bt_re_ref[0, d], bt_im_ref[0, d]
        bb_re = k_re * bt_re - k_im * bt_im
        bb_im = k_re * bt_im + k_im * bt_re
        magk = jnp.exp(krow * (lr * dt))
        ak_re = magk * jnp.cos(krow * (li * dt))
        ak_im = magk * jnp.sin(krow * (li * dt))
        ba_re = bb_re * ak_re - bb_im * ak_im
        ba_im = bb_re * ak_im + bb_im * ak_re
        ct_re, ct_im = ct_re_ref[0, d], ct_im_ref[0, d]
        k_tl.append(_dot_exact(jnp.where(first_lane, ba_re, 0.0), ct_re)
                    - _dot_exact(jnp.where(first_lane, ba_im, 0.0), ct_im))
        blocks = range(L - 1, -1, -1) if d == 0 else range(L)
        p_re = jnp.concatenate([ba_re[k * width:(k + 1) * width] for k in blocks], axis=0)
        p_im = jnp.concatenate([ba_im[k * width:(k + 1) * width] for k in blocks], axis=0)
        pp_ref[0, :, (2 * d) * lanes2:(2 * d + 1) * lanes2] = jnp.where(keep_lane, p_re, 0.0).astype(BF16)
        pp_ref[0, :, (2 * d + 1) * lanes2:(2 * d + 2) * lanes2] = jnp.where(keep_lane, p_im, 0.0).astype(BF16)
        lrc, lic, ldtc = colp_ref[0, d, :, 0:1], colp_ref[0, d, :, 1:2], colp_ref[0, d, :, 2:3]
        dtc = jnp.exp(ldtc)
        kq = (tl + 1).astype(F32) if d == 0 else (L - tl).astype(F32)
        magq = jnp.exp((lrc * dtc) * kq)
        aq_re = magq * jnp.cos((lic * dtc) * kq)
        aq_im = magq * jnp.sin((lic * dtc) * kq)
        q_re = ct_re * aq_re - ct_im * aq_im
        q_im = -(ct_re * aq_im + ct_im * aq_re)
        qq_ref[0, (2 * d) * lanes2:(2 * d + 1) * lanes2, :] = jnp.where(keep_sub, q_re, 0.0).astype(BF16)
        qq_ref[0, (2 * d + 1) * lanes2:(2 * d + 2) * lanes2, :] = jnp.where(keep_sub, q_im, 0.0).astype(BF16)
        magl = jnp.exp(float(L) * (lr * dt))
        dec_ref[0, 2 * d:2 * d + 1, :] = jnp.where(keep_lane, magl * jnp.cos(float(L) * (li * dt)), 0.0)
        dec_ref[0, 2 * d + 1:2 * d + 2, :] = jnp.where(keep_lane, magl * jnp.sin(float(L) * (li * dt)), 0.0)
    dec_ref[0, 4:8, :] = jnp.zeros((4, lanes2), F32)

    kf, kb = k_tl
    blk = lambda a, k: a[k * width:(k + 1) * width]
    diag = blk(kf, 0) + blk(kb, 0) + dt_ref[0]
    for s in range(L):
        acc = jnp.where(tl == s, diag, 0.0)
        for k in range(1, L - s):
            acc = jnp.where(tl == s + k, blk(kf, k), acc)
        for k in range(1, s + 1):
            acc = jnp.where(tl == s - k, blk(kb, k), acc)
        tt_ref[0, s * width:(s + 1) * width, :] = acc.astype(BF16)


def _ssm_prep(lam_re, lam_im, log_dt, b_re, b_im, c_re, c_im, d_skip):
    _, g, p, w = b_re.shape
    L = SSM_CHUNK
    rows = L * w
    dup = lambda a: jnp.concatenate([a, a], axis=-1)
    ldt = jnp.broadcast_to(log_dt[:, :, None], lam_re.shape)
    rowp = jnp.stack([dup(lam_re), dup(lam_im), dup(ldt)], axis=2)
    rowp = jnp.pad(rowp, ((0, 0), (0, 0), (0, SUBLANES - 3), (0, 0))).transpose(1, 0, 2, 3)
    colp = jnp.stack([dup(lam_re), dup(lam_im), dup(ldt)], axis=3)
    colp = jnp.pad(colp, ((0, 0), (0, 0), (0, 0), (0, LANES - 3))).transpose(1, 0, 2, 3)
    bt = lambda b: jnp.tile(dup(jnp.swapaxes(b, 2, 3)), (1, 1, L, 1)).transpose(1, 0, 2, 3)
    ct = lambda c: jnp.tile(jnp.concatenate([jnp.swapaxes(c, 2, 3)] * 2, axis=2),
                            (1, 1, 1, L)).transpose(1, 0, 2, 3)
    dmat = jnp.where(jnp.eye(w, dtype=bool)[None], d_skip.reshape(g, 1, w), 0.0)
    dtile = jnp.tile(dmat, (1, 1, L)).astype(F32)
    grp = lambda shape: pl.BlockSpec((1,) + shape, lambda i: (i,) + (0,) * len(shape))
    return pl.pallas_call(
        functools.partial(_ssm_prep_body, n_state=p),
        grid=(g,),
        in_specs=[grp((2, SUBLANES, 2 * p)), grp((2, 2 * p, LANES)), grp((2, rows, 2 * p)), grp((2, rows, 2 * p)),
                  grp((2, 2 * p, rows)), grp((2, 2 * p, rows)), grp((w, rows))],
        out_specs=[grp((rows, rows)), grp((rows, 8 * p)), grp((8 * p, rows)), grp((SUBLANES, 2 * p))],
        out_shape=[jax.ShapeDtypeStruct((g, rows, rows), BF16),
                   jax.ShapeDtypeStruct((g, rows, 8 * p), BF16),
                   jax.ShapeDtypeStruct((g, 8 * p, rows), BF16),
                   jax.ShapeDtypeStruct((g, SUBLANES, 2 * p), F32)],
        compiler_params=_cparams(("parallel",)),
        name="ssm_prep",
    )(rowp.astype(F32), colp.astype(F32), bt(b_re).astype(F32), bt(b_im).astype(F32),
      ct(c_re).astype(F32), ct(c_im).astype(F32), dtile)


def _ssm_body(u_ref, tt_ref, pp_ref, qq_ref, dec_ref, y_ref, st_ref, *, n_chunks):
    bsz = SUBLANES
    w = LANES
    st_ref[...] = _dot(u_ref[0], pp_ref[0]) + _dot(u_ref[1], pp_ref[1])
    dec = dec_ref[0] + dec_ref[1]
    a_fr, a_fi, a_br, a_bi = (jnp.broadcast_to(dec[i:i + 1, :], (bsz, w)) for i in range(4))

    def step(c, carry):
        hfr, hfi, hbr, hbi = carry
        rf = pl.multiple_of(c * bsz, bsz)
        rb = pl.multiple_of((n_chunks - 1 - c) * bsz, bsz)
        efr = st_ref[pl.ds(rf, bsz), 0 * w:1 * w]
        efi = st_ref[pl.ds(rf, bsz), 1 * w:2 * w]
        ebr = st_ref[pl.ds(rb, bsz), 2 * w:3 * w]
        ebi = st_ref[pl.ds(rb, bsz), 3 * w:4 * w]
        st_ref[pl.ds(rf, bsz), 0 * w:1 * w] = hfr
        st_ref[pl.ds(rf, bsz), 1 * w:2 * w] = hfi
        st_ref[pl.ds(rb, bsz), 2 * w:3 * w] = hbr
        st_ref[pl.ds(rb, bsz), 3 * w:4 * w] = hbi
        return (a_fr * hfr - a_fi * hfi + efr, a_fi * hfr + a_fr * hfi + efi,
                a_br * hbr - a_bi * hbi + ebr, a_bi * hbr + a_br * hbi + ebi)

    zero = jnp.zeros((bsz, w), F32)
    lax.fori_loop(0, n_chunks, step, (zero, zero, zero, zero))
    st = st_ref[...].astype(BF16)
    for g in range(2):
        y_ref[g] = (_dot(u_ref[g], tt_ref[g]) + _dot(st, qq_ref[g])).astype(BF16)


def _ssm(u_rows, tt, pp, qq, dec):
    g, r, k = u_rows.shape
    n_state8 = pp.shape[2]
    pair = lambda shape: pl.BlockSpec((2,) + shape, lambda i: (i,) + (0,) * len(shape))
    return pl.pallas_call(
        functools.partial(_ssm_body, n_chunks=r // SUBLANES),
        grid=(g // 2,),
        in_specs=[pair((r, k)), pair((k, k)), pair((k, n_state8)), pair((n_state8, k)),
                  pair((SUBLANES, dec.shape[2]))],
        out_specs=pair((r, k)),
        out_shape=jax.ShapeDtypeStruct((g, r, k), BF16),
        scratch_shapes=[pltpu.VMEM((r, n_state8), F32)],
        compiler_params=_cparams(("parallel",)),
        name="ssm",
    )(u_rows, tt, pp, qq, dec)


def _merge_body(h_ref, at_ref, y_ref, ga_ref, gs_ref, woa_ref, wglu_ref, bglu_ref, wos_ref, wout_ref, o_ref):
    ssm_w = y_ref.shape[2]
    attn = lax.dot_general(at_ref[0], woa_ref[...], (((0,), (0,)), ((), ())), preferred_element_type=F32)
    y = y_ref[0].astype(F32)
    glu = _dot(jax.nn.gelu(y).astype(BF16), wglu_ref[...]) + bglu_ref[...]
    ssm_out = glu[:, :ssm_w] * jax.nn.sigmoid(glu[:, ssm_w:])
    ssm = _dot(ssm_out.astype(BF16), wos_ref[...])
    merged = ga_ref[0].astype(F32) * attn + gs_ref[0].astype(F32) * ssm
    o_ref[0] = h_ref[0] + _dot(merged.astype(BF16), wout_ref[...])


def _merge(h1, attn_t, y_ssm, ga, gs, w_o_attn, w_glu, b_glu, w_o_ssm, w_out):
    b, s, d = h1.shape
    hv = attn_t.shape[1]
    ssm_w = y_ssm.shape[2]
    tm = min(TOKEN_TILE, s)
    const = lambda shape: pl.BlockSpec(shape, lambda i, j: (0,) * len(shape))
    tok = lambda w: pl.BlockSpec((1, tm, w), lambda i, j: (i, j, 0))
    return pl.pallas_call(
        _merge_body,
        grid=(b, s // tm),
        in_specs=[tok(d), pl.BlockSpec((1, hv, tm), lambda i, j: (i, 0, j)), tok(ssm_w), tok(d), tok(d),
                  const(w_o_attn.shape), const(w_glu.shape), const((1, w_glu.shape[1])),
                  const(w_o_ssm.shape), const(w_out.shape)],
        out_specs=tok(d),
        out_shape=jax.ShapeDtypeStruct((b, s, d), F32),
        compiler_params=_cparams(("parallel", "parallel")),
        name="merge",
    )(h1, attn_t, y_ssm, ga, gs, w_o_attn.astype(BF16), w_glu.astype(BF16),
      b_glu.reshape(1, -1).astype(F32), w_o_ssm.astype(BF16), w_out.astype(BF16))


def kernel(x, positions, ffn1_norm, ffn1_w_gate, ffn1_w_up, ffn1_w_down, mix_norm, w_in, q_norm, w_uq, kv_norm, w_ukv, w_o_attn, ssm_lambda_re, ssm_lambda_im, ssm_log_dt, ssm_b_re, ssm_b_im, ssm_c_re, ssm_c_im, ssm_d, w_glu, b_glu, w_o_ssm, w_out, ffn2_norm, ffn2_w_gate, ffn2_w_up, ffn2_w_down, final_norm):
    b, s, d = x.shape
    depth = ffn1_norm.shape[0]
    _, _, n_grp, _, grp_w = ssm_b_re.shape
    ssm_w = n_grp * grp_w
    L = SSM_CHUNK
    assert depth >= 1, "the final norm is fused into the last layer's second FFN"
    assert b == SUBLANES, "the S5 chunk recurrence keeps the batch on the 8 sublanes"
    assert s % max(TOKEN_TILE, ATTN_TQ) == 0 and L * grp_w == MXU_DIM and n_grp % 2 == 0
    n_chunks = s // L

    cos_c, sin_c = _rope_tables(positions)
    h = x
    for l in range(depth):
        last = l == depth - 1
        h = _ffn(h.reshape(b * s, d), ffn1_norm[l], ffn1_w_gate[l], ffn1_w_up[l], ffn1_w_down[l]).reshape(b, s, d)
        q, k, vt, u, ga, gs = _inproj(h, cos_c, sin_c, mix_norm[l], w_in[l], q_norm[l], w_uq[l],
                                      kv_norm[l], w_ukv[l], ssm_w)
        attn_t = _attention(q, k, vt)

        tt, pp, qq, dec = _ssm_prep(ssm_lambda_re[l], ssm_lambda_im[l], ssm_log_dt[l], ssm_b_re[l], ssm_b_im[l],
                                    ssm_c_re[l], ssm_c_im[l], ssm_d[l])
        u_rows = u.reshape(b, n_chunks, L, n_grp, grp_w).transpose(3, 1, 0, 2, 4).reshape(n_grp, n_chunks * b, L * grp_w)
        y_rows = _ssm(u_rows, tt, pp, qq, dec)
        y_ssm = y_rows.reshape(n_grp, n_chunks, b, L, grp_w).transpose(2, 1, 3, 0, 4).reshape(b, s, ssm_w)

        h = _merge(h, attn_t, y_ssm, ga, gs, w_o_attn[l], w_glu[l], b_glu[l], w_o_ssm[l], w_out[l])
        h = _ffn(h.reshape(b * s, d), ffn2_norm[l], ffn2_w_gate[l], ffn2_w_up[l], ffn2_w_down[l],
                 final_norm if last else None).reshape(b, s, d)
    return h
```

```python
import functools
import math

import jax
import jax.numpy as jnp
from jax import lax
from jax.experimental import pallas as pl
from jax.experimental.pallas import tpu as pltpu

F32 = jnp.float32
BF16 = jnp.bfloat16

MLA_HEADS = 8
QK_NOPE_DIM = 64
QK_ROPE_DIM = 32
V_HEAD_DIM = 64
ROPE_THETA = 10000.0
FFN_RES = 0.5
EPS = 1e-6

LANES = 128
SUBLANES = 8
MXU_DIM = 256
VMEM_LIMIT_BYTES = 56 * 1024 * 1024

HEAD_PAD = LANES
V_ONES_ROWS = 16
SSM_CHUNK = 16

TOKEN_TILE = 512
ATTN_TQ = 1024
ATTN_TK = 512
FF_CHUNK = 256

def _cparams(sem, flags=None):
    return pltpu.CompilerParams(dimension_semantics=sem, vmem_limit_bytes=VMEM_LIMIT_BYTES, flags=flags)


def _rms(x, g):
    return x * lax.rsqrt(jnp.mean(x * x, axis=-1, keepdims=True) + EPS) * g


def _dot(a, b):
    return jnp.dot(a, b, preferred_element_type=F32)


def _dot_nt(a, b):
    return lax.dot_general(a, b, (((1,), (1,)), ((), ())), preferred_element_type=F32)


def _dot_tn(a, b):
    return lax.dot_general(a, b, (((0,), (0,)), ((), ())), preferred_element_type=F32)


def _dot_exact(a, b):
    return jnp.dot(a, b, preferred_element_type=F32, precision=lax.Precision.HIGHEST)


def _rope_body(pos_ref, invf_ref, cos_ref, sin_ref):
    ang = pos_ref[...] * invf_ref[...]
    cos_ref[...] = jnp.cos(ang)
    sin_ref[...] = jnp.sin(ang)


def _rope_tables(positions):
    b, s = positions.shape
    half = QK_ROPE_DIM // 2
    per_row = LANES // half
    inv_freq = ROPE_THETA ** (-jnp.arange(0, QK_ROPE_DIM, 2, dtype=F32) / QK_ROPE_DIM)
    pos = jnp.repeat(positions.astype(F32), half, axis=-1).reshape(b, s // per_row, LANES)
    invf = jnp.tile(inv_freq, per_row).reshape(1, LANES)
    rows = s // per_row
    spec = pl.BlockSpec((1, rows, LANES), lambda i: (i, 0, 0))
    cos, sin = pl.pallas_call(
        _rope_body,
        grid=(b,),
        in_specs=[spec, pl.BlockSpec((1, LANES), lambda i: (0, 0))],
        out_specs=[spec, spec],
        out_shape=[jax.ShapeDtypeStruct((b, rows, LANES), F32)] * 2,
        compiler_params=_cparams(("parallel",)),
        name="rope_tables",
    )(pos, invf)
    return cos.reshape(b, s, half), sin.reshape(b, s, half)


def _ffn_body(x_ref, g_ref, wg_ref, wu_ref, wd_ref, *rest, n_chunks, final):
    o_ref = rest[-1]
    x = x_ref[...]
    n = _rms(x, g_ref[...]).astype(BF16)
    acc = jnp.zeros_like(x)
    for j in range(n_chunks):
        gate = _dot(n, wg_ref[j])
        up = _dot(n, wu_ref[j])
        act = (gate * jax.nn.sigmoid(gate) * up).astype(BF16)
        acc = acc + _dot(act, wd_ref[j])
    h = x + FFN_RES * acc
    if final:
        h = _rms(h, rest[0][...])
    o_ref[...] = h


def _ffn(x2, norm, w_gate, w_up, w_down, final_norm=None):
    t, d = x2.shape
    d_ff = w_gate.shape[1]
    n_chunks = d_ff // FF_CHUNK
    wg = w_gate.astype(BF16).reshape(d, n_chunks, FF_CHUNK).transpose(1, 0, 2)
    wu = w_up.astype(BF16).reshape(d, n_chunks, FF_CHUNK).transpose(1, 0, 2)
    wd = w_down.astype(BF16).reshape(n_chunks, FF_CHUNK, d)
    tm = min(TOKEN_TILE, t)
    row = pl.BlockSpec((tm, d), lambda i: (i, 0))
    vec = pl.BlockSpec((1, d), lambda i: (0, 0))
    wspec_in = pl.BlockSpec((n_chunks, d, FF_CHUNK), lambda i: (0, 0, 0))
    wspec_out = pl.BlockSpec((n_chunks, FF_CHUNK, d), lambda i: (0, 0, 0))
    args = [x2, norm.reshape(1, d).astype(F32), wg, wu, wd]
    in_specs = [row, vec, wspec_in, wspec_in, wspec_out]
    if final_norm is not None:
        args.append(final_norm.reshape(1, d).astype(F32))
        in_specs.append(vec)
    return pl.pallas_call(
        functools.partial(_ffn_body, n_chunks=n_chunks, final=final_norm is not None),
        grid=(t // tm,),
        in_specs=in_specs,
        out_specs=row,
        out_shape=jax.ShapeDtypeStruct((t, d), F32),
        compiler_params=_cparams(("parallel",)),
        name="ffn_final" if final_norm is not None else "ffn",
    )(*args)


def _inproj_body(h_ref, g_ref, win_ref, qn_ref, kvn_ref, wq_ref, wqr_ref, wk_ref, wvt_ref,
                 cos_ref, sin_ref, place_ref, base_ref,
                 q_ref, k_ref, vt_ref, u_ref, ga_ref, gs_ref, *, dims):
    q_rank, kv_rank, ssm_w, d_model, scale = dims
    n = _rms(h_ref[0], g_ref[...]).astype(BF16)
    z = _dot(n, win_ref[...])
    o = 0
    c_q = z[:, o:o + q_rank]; o += q_rank
    c_kv = z[:, o:o + kv_rank]; o += kv_rank
    kr = z[:, o:o + HEAD_PAD]; o += HEAD_PAD
    kr_rot = z[:, o:o + HEAD_PAD]; o += HEAD_PAD
    u = z[:, o:o + ssm_w]; o += ssm_w
    g_attn = z[:, o:o + d_model]; o += d_model
    g_ssm = z[:, o:o + d_model]

    u_ref[0] = u.astype(BF16)
    ga_ref[0] = jax.nn.sigmoid(g_attn).astype(BF16)
    gs_ref[0] = jax.nn.sigmoid(g_ssm).astype(BF16)

    cos = _dot_exact(cos_ref[0], place_ref[...]) + base_ref[...]
    sin = _dot_exact(sin_ref[0], place_ref[...])

    cqn = _rms(c_q, qn_ref[...]).astype(BF16)
    q_all = _dot(cqn, wq_ref[...])
    q_rot = _dot(cqn, wqr_ref[...])
    ckvn = _rms(c_kv, kvn_ref[...]).astype(BF16)
    k_nope = _dot(ckvn, wk_ref[...])
    k_rope = kr * cos + kr_rot * sin
    for h in range(MLA_HEADS):
        sl = slice(h * HEAD_PAD, (h + 1) * HEAD_PAD)
        q_ref[0, h] = ((q_all[:, sl] * cos + q_rot[:, sl] * sin) * scale).astype(BF16)
        k_ref[0, h] = (k_nope[:, sl] + k_rope).astype(BF16)
    vt = _dot_nt(wvt_ref[...], ckvn).astype(BF16)
    ones = jnp.ones((V_ONES_ROWS, vt.shape[1]), BF16)
    vt_ref[0, 0] = jnp.concatenate(
        [piece for h in range(MLA_HEADS) for piece in (vt[h * V_HEAD_DIM:(h + 1) * V_HEAD_DIM], ones)], axis=0)


def _place_cols(w, starts, width, total):
    out = jnp.zeros((w.shape[0], total), w.dtype)
    for i, s in enumerate(starts):
        out = lax.dynamic_update_slice(out, w[:, i * width:(i + 1) * width], (0, s))
    return out


def _rot_half(w):
    half = w.shape[1] // 2
    return jnp.concatenate([-w[:, half:], w[:, :half]], axis=1)


def _inproj(h1, cos_c, sin_c, mix_norm, w_in, q_norm, w_uq, kv_norm, w_ukv, ssm_w):
    b, s, d = h1.shape
    q_rank, kv_rank = q_norm.shape[0], kv_norm.shape[0]
    hq = QK_NOPE_DIM + QK_ROPE_DIM
    hkv = QK_NOPE_DIM + V_HEAD_DIM
    scale = float(hq) ** -0.5 * math.log2(math.e)
    o_kr = q_rank + kv_rank
    o_u = o_kr + QK_ROPE_DIM
    w_kr = w_in[:, o_kr:o_u]
    kr_p = _place_cols(w_kr, [QK_NOPE_DIM], QK_ROPE_DIM, HEAD_PAD)
    kr_rot_p = _place_cols(_rot_half(w_kr), [QK_NOPE_DIM], QK_ROPE_DIM, HEAD_PAD)
    win = jnp.concatenate([w_in[:, :o_kr], kr_p, kr_rot_p, w_in[:, o_u:]], axis=1).astype(BF16)
    wq = jnp.concatenate([jnp.pad(w_uq[:, h * hq:(h + 1) * hq], ((0, 0), (0, HEAD_PAD - hq)))
                          for h in range(MLA_HEADS)], axis=1).astype(BF16)
    wqr = jnp.concatenate(
        [_place_cols(_rot_half(w_uq[:, h * hq + QK_NOPE_DIM:(h + 1) * hq]), [QK_NOPE_DIM], QK_ROPE_DIM, HEAD_PAD)
         for h in range(MLA_HEADS)], axis=1).astype(BF16)
    wk = jnp.concatenate([jnp.pad(w_ukv[:, h * hkv:h * hkv + QK_NOPE_DIM], ((0, 0), (0, HEAD_PAD - QK_NOPE_DIM)))
                          for h in range(MLA_HEADS)], axis=1).astype(BF16)
    wvt = jnp.concatenate([w_ukv[:, h * hkv + QK_NOPE_DIM:(h + 1) * hkv] for h in range(MLA_HEADS)],
                          axis=1).T.astype(BF16)
    half = QK_ROPE_DIM // 2
    eye = jnp.eye(half, dtype=F32)
    place = _place_cols(jnp.concatenate([eye, eye], axis=1), [QK_NOPE_DIM], QK_ROPE_DIM, HEAD_PAD)
    base = (jnp.arange(HEAD_PAD) < QK_NOPE_DIM).astype(F32).reshape(1, HEAD_PAD)

    tm = min(TOKEN_TILE, s)
    nt = s // tm
    hv = MLA_HEADS * (V_HEAD_DIM + V_ONES_ROWS)
    const = lambda shape: pl.BlockSpec(shape, lambda i, j: (0,) * len(shape))
    tok = lambda w: pl.BlockSpec((1, tm, w), lambda i, j: (i, j, 0))
    heads = pl.BlockSpec((1, MLA_HEADS, tm, HEAD_PAD), lambda i, j: (i, 0, j, 0))
    dims = (q_rank, kv_rank, ssm_w, d, scale)
    return pl.pallas_call(
        functools.partial(_inproj_body, dims=dims),
        grid=(b, nt),
        in_specs=[tok(d), const((1, d)), const(win.shape), const((1, q_rank)), const((1, kv_rank)),
                  const(wq.shape), const(wqr.shape), const(wk.shape), const(wvt.shape),
                  tok(half), tok(half), const(place.shape), const(base.shape)],
        out_specs=[heads, heads, pl.BlockSpec((1, 1, hv, tm), lambda i, j: (i, j, 0, 0)),
                   tok(ssm_w), tok(d), tok(d)],
        out_shape=[jax.ShapeDtypeStruct((b, MLA_HEADS, s, HEAD_PAD), BF16),
                   jax.ShapeDtypeStruct((b, MLA_HEADS, s, HEAD_PAD), BF16),
                   jax.ShapeDtypeStruct((b, nt, hv, tm), BF16),
                   jax.ShapeDtypeStruct((b, s, ssm_w), BF16),
                   jax.ShapeDtypeStruct((b, s, d), BF16),
                   jax.ShapeDtypeStruct((b, s, d), BF16)],
        compiler_params=_cparams(("parallel", "parallel")),
        name="inproj",
    )(h1, mix_norm.reshape(1, d).astype(F32), win, q_norm.reshape(1, q_rank).astype(F32),
      kv_norm.reshape(1, kv_rank).astype(F32), wq, wqr, wk, wvt, cos_c, sin_c, place, base)


def _attn_body(q_ref, k_ref, vt_ref, o_ref, s0_ref, s1_ref, acc_ref, *, n_kt, tk):
    q = q_ref[0, 0]
    tq = q.shape[0]
    s_ref = (s0_ref, s1_ref)

    def scores(j, slot):
        start = pl.multiple_of(j * tk, tk)
        s_ref[slot][...] = _dot_nt(k_ref[0, 0, pl.ds(start, tk), :], q)

    def consume(j, slot, m):
        s = s_ref[slot][...]
        m_new = jnp.maximum(m, jnp.max(s, axis=0, keepdims=True))
        p = jnp.exp2(s - m_new).astype(BF16)
        acc_ref[...] = jnp.exp2(m - m_new) * acc_ref[...] + _dot(vt_ref[0, j], p)
        return m_new

    def pair(i, m):
        scores(2 * i + 1, 1)
        m = consume(2 * i, 0, m)
        scores(2 * i + 2, 0)
        return consume(2 * i + 1, 1, m)

    assert n_kt == 1 or n_kt % 2 == 0
    acc_ref[...] = jnp.zeros_like(acc_ref)
    m = jnp.full((1, tq), -jnp.inf, F32)
    scores(0, 0)
    if n_kt > 1:
        for i in range(n_kt // 2 - 1):
            m = pair(i, m)
        scores(n_kt - 1, 1)
        m = consume(n_kt - 2, 0, m)
        consume(n_kt - 1, 1, m)
    else:
        consume(0, 0, m)
    acc = acc_ref[...]
    o_ref[0] = (acc[:V_HEAD_DIM] / acc[V_HEAD_DIM:V_HEAD_DIM + 1]).astype(BF16)


def _attention(q, k, vt):
    b, h, s, _ = q.shape
    n_kt, tk = vt.shape[1], vt.shape[3]
    vrows = vt.shape[2] // h
    tq = min(ATTN_TQ, s)
    return pl.pallas_call(
        functools.partial(_attn_body, n_kt=n_kt, tk=tk),
        grid=(b, h, s // tq),
        in_specs=[pl.BlockSpec((1, 1, tq, HEAD_PAD), lambda i, j, t: (i, j, t, 0)),
                  pl.BlockSpec((1, 1, s, HEAD_PAD), lambda i, j, t: (i, j, 0, 0)),
                  pl.BlockSpec((1, n_kt, vrows, tk), lambda i, j, t: (i, 0, j, 0))],
        out_specs=pl.BlockSpec((1, V_HEAD_DIM, tq), lambda i, j, t: (i, j, t)),
        out_shape=jax.ShapeDtypeStruct((b, h * V_HEAD_DIM, s), BF16),
        scratch_shapes=[pltpu.VMEM((tk, tq), F32), pltpu.VMEM((tk, tq), F32), pltpu.VMEM((vrows, tq), F32)],
        compiler_params=_cparams(("parallel", "parallel", "arbitrary")),
        name="attention",
    )(q, k, vt)


def _discretise(lr, li, ldt):
    dt = jnp.exp(ldt)
    mag = jnp.exp(lr * dt)
    lb_re = mag * jnp.cos(li * dt)
    lb_im = mag * jnp.sin(li * dt)
    den = lr * lr + li * li
    nr = lb_re - 1.0
    return (nr * lr + lb_im * li) / den, (lb_im * lr - nr * li) / den, lr * dt, li * dt


def _ssm_prep_body(rowp_ref, colp_ref, bt_re_ref, bt_im_ref, bc_re_ref, bc_im_ref, cr_re_ref, cr_im_ref, dt_ref,
                   tt_ref, pp_ref, qq_ref, dec_ref, *, n_state):
    L = SSM_CHUNK
    rows = tt_ref.shape[1]
    width = rows // L
    lanes2 = 2 * n_state
    parity = pl.program_id(0) % 2
    lane = lax.broadcasted_iota(jnp.int32, (1, lanes2), 1)
    keep_lane = (lane >= parity * n_state) & (lane < (parity + 1) * n_state)
    first_lane = lane < n_state
    krow = (lax.broadcasted_iota(jnp.int32, (rows + width, 1), 0) // width).astype(F32)
    sl = lax.broadcasted_iota(jnp.int32, (1, rows), 1) // width
    blk = lambda a, k: a[k * width:(k + 1) * width]

    k_rows = []
    for d in range(2):
        k_re, k_im, ar, ai = _discretise(rowp_ref[0, d, 0:1, :], rowp_ref[0, d, 1:2, :], rowp_ref[0, d, 2:3, :])
        magk = jnp.exp(krow * ar)
        ak_re = magk * jnp.cos(krow * ai)
        ak_im = magk * jnp.sin(krow * ai)
        bt_re, bt_im = bt_re_ref[0, d], bt_im_ref[0, d]
        bb_re = k_re * bt_re - k_im * bt_im
        bb_im = k_re * bt_im + k_im * bt_re
        ba_re = bb_re * ak_re[:rows] - bb_im * ak_im[:rows]
        ba_im = bb_re * ak_im[:rows] + bb_im * ak_re[:rows]
        blocks = range(L - 1, -1, -1) if d == 0 else range(L)
        p_re = jnp.concatenate([blk(ba_re, k) for k in blocks], axis=0)
        p_im = jnp.concatenate([blk(ba_im, k) for k in blocks], axis=0)
        pp_ref[0, :, (2 * d) * lanes2:(2 * d + 1) * lanes2] = jnp.where(keep_lane, p_re, 0.0).astype(BF16)
        pp_ref[0, :, (2 * d + 1) * lanes2:(2 * d + 2) * lanes2] = jnp.where(keep_lane, p_im, 0.0).astype(BF16)
        cr_re, cr_im = cr_re_ref[0, d], cr_im_ref[0, d]
        ca_re = cr_re * ak_re - cr_im * ak_im
        ca_im = cr_re * ak_im + cr_im * ak_re
        blocks = range(1, L + 1) if d == 0 else range(L, 0, -1)
        q_re = jnp.concatenate([blk(ca_re, k) for k in blocks], axis=0)
        q_im = jnp.concatenate([blk(ca_im, k) for k in blocks], axis=0)
        qq_ref[0, :, (2 * d) * lanes2:(2 * d + 1) * lanes2] = jnp.where(keep_lane, q_re, 0.0).astype(BF16)
        qq_ref[0, :, (2 * d + 1) * lanes2:(2 * d + 2) * lanes2] = jnp.where(keep_lane, -q_im, 0.0).astype(BF16)
        kc_re, kc_im, _, _ = _discretise(colp_ref[0, d, :, 0:1], colp_ref[0, d, :, 1:2], colp_ref[0, d, :, 2:3])
        bc_re, bc_im = bc_re_ref[0, d], bc_im_ref[0, d]
        bbc_re = kc_re * bc_re - kc_im * bc_im
        bbc_im = kc_re * bc_im + kc_im * bc_re
        k_rows.append(_dot_exact(jnp.where(first_lane, ca_re[:rows], 0.0), bbc_re)
                      - _dot_exact(jnp.where(first_lane, ca_im[:rows], 0.0), bbc_im))
        dec_ref[0, 2 * d:2 * d + 1, :] = jnp.where(keep_lane, ak_re[rows:rows + 1], 0.0)
        dec_ref[0, 2 * d + 1:2 * d + 2, :] = jnp.where(keep_lane, ak_im[rows:rows + 1], 0.0)
    dec_ref[0, 4:8, :] = jnp.zeros((4, lanes2), F32)

    kf, kb = k_rows
    diag = blk(kf, 0) + blk(kb, 0) + dt_ref[0]
    for t in range(L):
        acc = jnp.where(sl == t, diag, 0.0)
        for k in range(1, t + 1):
            acc = jnp.where(sl == t - k, blk(kf, k), acc)
        for k in range(1, L - t):
            acc = jnp.where(sl == t + k, blk(kb, k), acc)
        tt_ref[0, t * width:(t + 1) * width, :] = acc.astype(BF16)


def _ssm_prep(lam_re, lam_im, log_dt, b_re, b_im, c_re, c_im, d_skip):
    _, g, p, w = b_re.shape
    L = SSM_CHUNK
    rows = L * w
    dup = lambda a, axis: jnp.concatenate([a, a], axis=axis)
    ldt = jnp.broadcast_to(log_dt[:, :, None], lam_re.shape)
    params = [dup(lam_re, -1), dup(lam_im, -1), dup(ldt, -1)]
    rowp = jnp.pad(jnp.stack(params, axis=2), ((0, 0), (0, 0), (0, SUBLANES - 3), (0, 0)))
    colp = jnp.pad(jnp.stack(params, axis=3), ((0, 0), (0, 0), (0, 0), (0, LANES - 3)))
    per_group = lambda a: jnp.swapaxes(a, 0, 1).astype(F32)
    bt = lambda b: jnp.tile(dup(jnp.swapaxes(b, 2, 3), -1), (1, 1, L, 1))
    bc = lambda b: jnp.tile(dup(b, 2), (1, 1, 1, L))
    cr = lambda c: jnp.tile(dup(c, -1), (1, 1, L + 1, 1))
    dmat = jnp.where(jnp.eye(w, dtype=bool)[None], d_skip.reshape(g, 1, w), 0.0)
    dtile = jnp.tile(dmat, (1, 1, L)).astype(F32)
    grp = lambda shape: pl.BlockSpec((1,) + shape, lambda i: (i,) + (0,) * len(shape))
    return pl.pallas_call(
        functools.partial(_ssm_prep_body, n_state=p),
        grid=(g,),
        in_specs=[grp((2, SUBLANES, 2 * p)), grp((2, 2 * p, LANES)), grp((2, rows, 2 * p)), grp((2, rows, 2 * p)),
                  grp((2, 2 * p, rows)), grp((2, 2 * p, rows)), grp((2, rows + w, 2 * p)), grp((2, rows + w, 2 * p)),
                  grp((w, rows))],
        out_specs=[grp((rows, rows)), grp((rows, 8 * p)), grp((rows, 8 * p)), grp((SUBLANES, 2 * p))],
        out_shape=[jax.ShapeDtypeStruct((g, rows, rows), BF16),
                   jax.ShapeDtypeStruct((g, rows, 8 * p), BF16),
                   jax.ShapeDtypeStruct((g, rows, 8 * p), BF16),
                   jax.ShapeDtypeStruct((g, SUBLANES, 2 * p), F32)],
        compiler_params=_cparams(("parallel",)),
        name="ssm_prep",
    )(per_group(rowp), per_group(colp), per_group(bt(b_re)), per_group(bt(b_im)), per_group(bc(b_re)),
      per_group(bc(b_im)), per_group(cr(c_re)), per_group(cr(c_im)), dtile)


def _utile_body(u_ref, eye_ref, o_ref, scr_ref):
    scr_ref[...] = u_ref[0].astype(F32)
    half = SSM_CHUNK // 2
    for sh in range(2):
        rows = jnp.concatenate([scr_ref[:, sh * half + s, :] for s in range(half)], axis=0).astype(BF16)
        o_ref[0, sh] = _dot_nt(eye_ref[...], rows).astype(BF16)


def _utile(u):
    b, s, w = u.shape
    L = SSM_CHUNK
    n_cb = s // (L * LANES)
    u4 = u.reshape(b, s // L, L, w)
    eye = jnp.eye(w, dtype=BF16)
    return pl.pallas_call(
        _utile_body,
        grid=(b, n_cb),
        in_specs=[pl.BlockSpec((1, LANES, L, w), lambda i, j: (i, j, 0, 0)),
                  pl.BlockSpec((w, w), lambda i, j: (0, 0))],
        out_specs=pl.BlockSpec((1, 2, w, LANES * L // 2), lambda i, j: (i, j, 0, 0)),
        out_shape=jax.ShapeDtypeStruct((b, 2 * n_cb, w, LANES * L // 2), BF16),
        scratch_shapes=[pltpu.VMEM((LANES, L, w), F32)],
        compiler_params=_cparams(("parallel", "parallel")),
        name="ssm_utile",
    )(u4, eye)


def _ssm_body(u_ref, tt_ref, pp_ref, qq_ref, dec_ref, y_ref, st_ref, *, n_chunks):
    bsz = u_ref.shape[0]
    n_cb = u_ref.shape[1] // 2
    width = u_ref.shape[2] // 2
    L = SSM_CHUNK
    half = L // 2
    w = LANES
    assert bsz == SUBLANES

    def tile_of(cb, step):
        return 2 * cb + step // half, slice((step % half) * w, (step % half + 1) * w)

    def ucol(b, g):
        pieces = []
        for s in range(L):
            row = []
            for cb in range(n_cb):
                tile, lanes = tile_of(cb, s)
                row.append(u_ref[b, tile, g * width:(g + 1) * width, lanes])
            pieces.append(jnp.concatenate(row, axis=1))
        return jnp.concatenate(pieces, axis=0)

    for b in range(bsz):
        e = _dot_tn(ucol(b, 0), pp_ref[0]) + _dot_tn(ucol(b, 1), pp_ref[1])
        for i in range(4):
            st_ref[i, pl.ds(b, n_chunks, stride=bsz), :] = e[:, i * w:(i + 1) * w]

    dec = dec_ref[0] + dec_ref[1]
    a_fr, a_fi, a_br, a_bi = (jnp.broadcast_to(dec[i:i + 1, :], (bsz, w)) for i in range(4))

    def step(c, carry):
        hfr, hfi, hbr, hbi = carry
        rf = pl.multiple_of(c * bsz, bsz)
        rb = pl.multiple_of((n_chunks - 1 - c) * bsz, bsz)
        efr = st_ref[0, pl.ds(rf, bsz), :]
        efi = st_ref[1, pl.ds(rf, bsz), :]
        ebr = st_ref[2, pl.ds(rb, bsz), :]
        ebi = st_ref[3, pl.ds(rb, bsz), :]
        st_ref[0, pl.ds(rf, bsz), :] = hfr
        st_ref[1, pl.ds(rf, bsz), :] = hfi
        st_ref[2, pl.ds(rb, bsz), :] = hbr
        st_ref[3, pl.ds(rb, bsz), :] = hbi
        return (a_fr * hfr - a_fi * hfi + efr, a_fi * hfr + a_fr * hfi + efi,
                a_br * hbr - a_bi * hbi + ebr, a_bi * hbr + a_br * hbi + ebi)

    zero = jnp.zeros((bsz, w), F32)
    lax.fori_loop(0, n_chunks, step, (zero, zero, zero, zero))

    for b in range(bsz):
        h = jnp.concatenate([st_ref[i, pl.ds(b, n_chunks, stride=bsz), :] for i in range(4)],
                            axis=1).astype(BF16)
        for g in range(2):
            y = (_dot(tt_ref[g], ucol(b, g)) + _dot_nt(qq_ref[g], h)).astype(BF16)
            for t in range(L):
                for cb in range(n_cb):
                    tile, lanes = tile_of(cb, t)
                    y_ref[b, tile, g * width:(g + 1) * width, lanes] = y[t * width:(t + 1) * width, cb * w:(cb + 1) * w]


def _ssm(u_t, tt, pp, qq, dec):
    b, n_tiles, ssm_w, tl = u_t.shape
    g, k, _ = tt.shape
    n_state8 = pp.shape[2]
    width = k // SSM_CHUNK
    n_chunks = n_tiles // 2 * LANES
    pair = lambda shape: pl.BlockSpec((2,) + shape, lambda i: (i,) + (0,) * len(shape))
    seq = pl.BlockSpec((b, n_tiles, 2 * width, tl), lambda i: (0, 0, i, 0))
    return pl.pallas_call(
        functools.partial(_ssm_body, n_chunks=n_chunks),
        grid=(g // 2,),
        in_specs=[seq, pair((k, k)), pair((k, n_state8)), pair((k, n_state8)), pair((SUBLANES, dec.shape[2]))],
        out_specs=seq,
        out_shape=jax.ShapeDtypeStruct(u_t.shape, BF16),
        scratch_shapes=[pltpu.VMEM((n_state8 // LANES, n_chunks * b, LANES), F32)],
        compiler_params=_cparams(("parallel",)),
        name="ssm",
    )(u_t, tt, pp, qq, dec)


def _ssm_post_body(y_ref, wglu_ref, bglu_ref, wos_ref, o_ref):
    ssm_w = y_ref.shape[2]
    y = y_ref[0, 0].astype(F32)
    glu = _dot_tn(jax.nn.gelu(y).astype(BF16), wglu_ref[...]) + bglu_ref[...]
    ssm_out = glu[:, :ssm_w] * jax.nn.sigmoid(glu[:, ssm_w:])
    out = _dot(ssm_out.astype(BF16), wos_ref[...])
    for s in range(SSM_CHUNK // 2):
        o_ref[0, :, s, :] = out[s * LANES:(s + 1) * LANES]


def _ssm_post(y_t, w_glu, b_glu, w_o_ssm):
    b, n_tiles, ssm_w, tl = y_t.shape
    d = w_o_ssm.shape[1]
    L = SSM_CHUNK
    s = n_tiles * tl
    const = lambda shape: pl.BlockSpec(shape, lambda i, j: (0,) * len(shape))
    out = pl.pallas_call(
        _ssm_post_body,
        grid=(b, n_tiles),
        in_specs=[pl.BlockSpec((1, 1, ssm_w, tl), lambda i, j: (i, j, 0, 0)),
                  const(w_glu.shape), const((1, w_glu.shape[1])), const(w_o_ssm.shape)],
        out_specs=pl.BlockSpec((1, LANES, L // 2, d), lambda i, j: (i, j // 2, j % 2, 0)),
        out_shape=jax.ShapeDtypeStruct((b, s // L, L, d), F32),
        compiler_params=_cparams(("parallel", "parallel")),
        name="ssm_post",
    )(y_t, w_glu.astype(BF16), b_glu.reshape(1, -1).astype(F32), w_o_ssm.astype(BF16))
    return out.reshape(b, s, d)


def _merge_body(h_ref, at_ref, ssm_ref, ga_ref, gs_ref, woa_ref, wout_ref, o_ref):
    attn = _dot_tn(at_ref[0], woa_ref[...])
    merged = ga_ref[0].astype(F32) * attn + gs_ref[0].astype(F32) * ssm_ref[0]
    o_ref[0] = h_ref[0] + _dot(merged.astype(BF16), wout_ref[...])


def _merge(h1, attn_t, ssm, ga, gs, w_o_attn, w_out):
    b, s, d = h1.shape
    hv = attn_t.shape[1]
    tm = min(TOKEN_TILE, s)
    const = lambda shape: pl.BlockSpec(shape, lambda i, j: (0,) * len(shape))
    tok = lambda w: pl.BlockSpec((1, tm, w), lambda i, j: (i, j, 0))
    return pl.pallas_call(
        _merge_body,
        grid=(b, s // tm),
        in_specs=[tok(d), pl.BlockSpec((1, hv, tm), lambda i, j: (i, 0, j)), tok(d), tok(d), tok(d),
                  const(w_o_attn.shape), const(w_out.shape)],
        out_specs=tok(d),
        out_shape=jax.ShapeDtypeStruct((b, s, d), F32),
        compiler_params=_cparams(("parallel", "parallel")),
        name="merge",
    )(h1, attn_t, ssm, ga, gs, w_o_attn.astype(BF16), w_out.astype(BF16))


def kernel(x, positions, ffn1_norm, ffn1_w_gate, ffn1_w_up, ffn1_w_down, mix_norm, w_in, q_norm, w_uq, kv_norm, w_ukv, w_o_attn, ssm_lambda_re, ssm_lambda_im, ssm_log_dt, ssm_b_re, ssm_b_im, ssm_c_re, ssm_c_im, ssm_d, w_glu, b_glu, w_o_ssm, w_out, ffn2_norm, ffn2_w_gate, ffn2_w_up, ffn2_w_down, final_norm):
    b, s, d = x.shape
    depth = ffn1_norm.shape[0]
    _, _, n_grp, _, grp_w = ssm_b_re.shape
    ssm_w = n_grp * grp_w
    L = SSM_CHUNK
    assert depth >= 1, "the final norm is fused into the last layer's second FFN"
    assert b == SUBLANES, "the S5 chunk recurrence keeps the batch on the 8 sublanes"
    assert s % max(TOKEN_TILE, ATTN_TQ, L * LANES) == 0 and L * grp_w == MXU_DIM and n_grp % 2 == 0

    cos_c, sin_c = _rope_tables(positions)
    h = x
    for l in range(depth):
        last = l == depth - 1
        h = _ffn(h.reshape(b * s, d), ffn1_norm[l], ffn1_w_gate[l], ffn1_w_up[l], ffn1_w_down[l]).reshape(b, s, d)
        q, k, vt, u, ga, gs = _inproj(h, cos_c, sin_c, mix_norm[l], w_in[l], q_norm[l], w_uq[l],
                                      kv_norm[l], w_ukv[l], ssm_w)
        attn_t = _attention(q, k, vt)

        tt, pp, qq, dec = _ssm_prep(ssm_lambda_re[l], ssm_lambda_im[l], ssm_log_dt[l], ssm_b_re[l], ssm_b_im[l],
                                    ssm_c_re[l], ssm_c_im[l], ssm_d[l])
        y_t = _ssm(_utile(u), tt, pp, qq, dec)
        ssm = _ssm_post(y_t, w_glu[l], b_glu[l], w_o_ssm[l])

        h = _merge(h, attn_t, ssm, ga, gs, w_o_attn[l], w_out[l])
        h = _ffn(h.reshape(b * s, d), ffn2_norm[l], ffn2_w_gate[l], ffn2_w_up[l], ffn2_w_down[l],
                 final_norm if last else None).reshape(b, s, d)
    return h
```

```python
import functools
import math

import jax
import jax.numpy as jnp
from jax import lax
from jax.experimental import pallas as pl
from jax.experimental.pallas import tpu as pltpu

F32 = jnp.float32
BF16 = jnp.bfloat16

MLA_HEADS = 8
QK_NOPE_DIM = 64
QK_ROPE_DIM = 32
V_HEAD_DIM = 64
ROPE_THETA = 10000.0
FFN_RES = 0.5
EPS = 1e-6

LANES = 128
SUBLANES = 8
MXU_DIM = 256
VMEM_LIMIT_BYTES = 56 * 1024 * 1024

HEAD_PAD = LANES
V_ONES_ROWS = 16
SSM_CHUNK = 16

TOKEN_TILE = 512
ATTN_TQ = 1024
ATTN_TK = 512
FF_CHUNK = 256
ATTN_MAX_GAP = 64.0

def _cparams(sem, flags=None):
    return pltpu.CompilerParams(dimension_semantics=sem, vmem_limit_bytes=VMEM_LIMIT_BYTES, flags=flags)


def _rms(x, g):
    return x * lax.rsqrt(jnp.mean(x * x, axis=-1, keepdims=True) + EPS) * g


def _dot(a, b):
    return jnp.dot(a, b, preferred_element_type=F32)


def _dot_nt(a, b):
    return lax.dot_general(a, b, (((1,), (1,)), ((), ())), preferred_element_type=F32)


def _dot_tn(a, b):
    return lax.dot_general(a, b, (((0,), (0,)), ((), ())), preferred_element_type=F32)


def _dot_exact(a, b):
    return jnp.dot(a, b, preferred_element_type=F32, precision=lax.Precision.HIGHEST)


def _rope_body(pos_ref, invf_ref, cos_ref, sin_ref):
    ang = pos_ref[...] * invf_ref[...]
    cos_ref[...] = jnp.cos(ang)
    sin_ref[...] = jnp.sin(ang)


def _rope_tables(positions):
    b, s = positions.shape
    half = QK_ROPE_DIM // 2
    per_row = LANES // half
    inv_freq = ROPE_THETA ** (-jnp.arange(0, QK_ROPE_DIM, 2, dtype=F32) / QK_ROPE_DIM)
    pos = jnp.repeat(positions.astype(F32), half, axis=-1).reshape(b, s // per_row, LANES)
    invf = jnp.tile(inv_freq, per_row).reshape(1, LANES)
    rows = s // per_row
    spec = pl.BlockSpec((1, rows, LANES), lambda i: (i, 0, 0))
    cos, sin = pl.pallas_call(
        _rope_body,
        grid=(b,),
        in_specs=[spec, pl.BlockSpec((1, LANES), lambda i: (0, 0))],
        out_specs=[spec, spec],
        out_shape=[jax.ShapeDtypeStruct((b, rows, LANES), F32)] * 2,
        compiler_params=_cparams(("parallel",)),
        name="rope_tables",
    )(pos, invf)
    return cos.reshape(b, s, half), sin.reshape(b, s, half)


def _ffn_apply(x, g_ref, wg_ref, wu_ref, wd_ref):
    n = _rms(x, g_ref[...]).astype(BF16)
    acc = jnp.zeros_like(x)
    for j in range(wg_ref.shape[0]):
        gate = _dot(n, wg_ref[j])
        up = _dot(n, wu_ref[j])
        act = (gate * jax.nn.sigmoid(gate) * up).astype(BF16)
        acc = acc + _dot(act, wd_ref[j])
    return x + FFN_RES * acc


def _ffn_body(x_ref, g_ref, wg_ref, wu_ref, wd_ref, o_ref):
    o_ref[...] = _ffn_apply(x_ref[...], g_ref, wg_ref, wu_ref, wd_ref)


def _resident(shape):
    return pl.BlockSpec(shape, lambda *_: (0,) * len(shape), pipeline_mode=pl.Buffered(1))


def _ffn_operands(norm, w_gate, w_up, w_down):
    d, d_ff = w_gate.shape
    n_chunks = d_ff // FF_CHUNK
    wg = w_gate.astype(BF16).reshape(d, n_chunks, FF_CHUNK).transpose(1, 0, 2)
    wu = w_up.astype(BF16).reshape(d, n_chunks, FF_CHUNK).transpose(1, 0, 2)
    wd = w_down.astype(BF16).reshape(n_chunks, FF_CHUNK, d)
    args = [norm.reshape(1, d).astype(F32), wg, wu, wd]
    return args, [_resident(a.shape) for a in args]


def _ffn(x2, norm, w_gate, w_up, w_down):
    t, d = x2.shape
    tm = min(TOKEN_TILE, t)
    row = pl.BlockSpec((tm, d), lambda i: (i, 0))
    args, specs = _ffn_operands(norm, w_gate, w_up, w_down)
    return pl.pallas_call(
        _ffn_body,
        grid=(t // tm,),
        in_specs=[row] + specs,
        out_specs=row,
        out_shape=jax.ShapeDtypeStruct((t, d), F32),
        compiler_params=_cparams(("parallel",)),
        name="ffn",
    )(x2, *args)


def _inproj_body(h_ref, g_ref, win_ref, qn_ref, kvn_ref, wq_ref, wqr_ref, wk_ref, wvt_ref,
                 cos_ref, sin_ref, place_ref, base_ref,
                 q_ref, k_ref, vt_ref, u_ref, ga_ref, gs_ref, *, dims):
    q_rank, kv_rank, ssm_w, d_model, scale = dims
    n = _rms(h_ref[0], g_ref[...]).astype(BF16)
    z = _dot(n, win_ref[...])
    o = 0
    c_q = z[:, o:o + q_rank]; o += q_rank
    c_kv = z[:, o:o + kv_rank]; o += kv_rank
    kr = z[:, o:o + HEAD_PAD]; o += HEAD_PAD
    kr_rot = z[:, o:o + HEAD_PAD]; o += HEAD_PAD
    u = z[:, o:o + ssm_w]; o += ssm_w
    g_attn = z[:, o:o + d_model]; o += d_model
    g_ssm = z[:, o:o + d_model]

    u_ref[0] = u.astype(BF16)
    ga_ref[0] = jax.nn.sigmoid(g_attn).astype(BF16)
    gs_ref[0] = jax.nn.sigmoid(g_ssm).astype(BF16)

    cos = _dot_exact(cos_ref[0], place_ref[...]) + base_ref[...]
    sin = _dot_exact(sin_ref[0], place_ref[...])

    cqn = _rms(c_q, qn_ref[...]).astype(BF16)
    q_all = _dot(cqn, wq_ref[...])
    q_rot = _dot(cqn, wqr_ref[...])
    ckvn = _rms(c_kv, kvn_ref[...]).astype(BF16)
    k_nope = _dot(ckvn, wk_ref[...])
    k_rope = kr * cos + kr_rot * sin
    for h in range(MLA_HEADS):
        sl = slice(h * HEAD_PAD, (h + 1) * HEAD_PAD)
        q_ref[0, h] = ((q_all[:, sl] * cos + q_rot[:, sl] * sin) * scale).astype(BF16)
        k_ref[0, h] = (k_nope[:, sl] + k_rope).astype(BF16)
    vt = _dot_nt(wvt_ref[...], ckvn).astype(BF16)
    ones = jnp.ones((V_ONES_ROWS, vt.shape[1]), BF16)
    vt_ref[0, 0] = jnp.concatenate(
        [piece for h in range(MLA_HEADS) for piece in (vt[h * V_HEAD_DIM:(h + 1) * V_HEAD_DIM], ones)], axis=0)


def _place_cols(w, starts, width, total):
    out = jnp.zeros((w.shape[0], total), w.dtype)
    for i, s in enumerate(starts):
        out = lax.dynamic_update_slice(out, w[:, i * width:(i + 1) * width], (0, s))
    return out


def _rot_half(w):
    half = w.shape[1] // 2
    return jnp.concatenate([-w[:, half:], w[:, :half]], axis=1)


def _inproj(h1, cos_c, sin_c, mix_norm, w_in, q_norm, w_uq, kv_norm, w_ukv, ssm_w):
    b, s, d = h1.shape
    q_rank, kv_rank = q_norm.shape[0], kv_norm.shape[0]
    hq = QK_NOPE_DIM + QK_ROPE_DIM
    hkv = QK_NOPE_DIM + V_HEAD_DIM
    scale = float(hq) ** -0.5 * math.log2(math.e)
    o_kr = q_rank + kv_rank
    o_u = o_kr + QK_ROPE_DIM
    w_kr = w_in[:, o_kr:o_u]
    kr_p = _place_cols(w_kr, [QK_NOPE_DIM], QK_ROPE_DIM, HEAD_PAD)
    kr_rot_p = _place_cols(_rot_half(w_kr), [QK_NOPE_DIM], QK_ROPE_DIM, HEAD_PAD)
    win = jnp.concatenate([w_in[:, :o_kr], kr_p, kr_rot_p, w_in[:, o_u:]], axis=1).astype(BF16)
    wq = jnp.concatenate([jnp.pad(w_uq[:, h * hq:(h + 1) * hq], ((0, 0), (0, HEAD_PAD - hq)))
                          for h in range(MLA_HEADS)], axis=1).astype(BF16)
    wqr = jnp.concatenate(
        [_place_cols(_rot_half(w_uq[:, h * hq + QK_NOPE_DIM:(h + 1) * hq]), [QK_NOPE_DIM], QK_ROPE_DIM, HEAD_PAD)
         for h in range(MLA_HEADS)], axis=1).astype(BF16)
    wk = jnp.concatenate([jnp.pad(w_ukv[:, h * hkv:h * hkv + QK_NOPE_DIM], ((0, 0), (0, HEAD_PAD - QK_NOPE_DIM)))
                          for h in range(MLA_HEADS)], axis=1).astype(BF16)
    wvt = jnp.concatenate([w_ukv[:, h * hkv + QK_NOPE_DIM:(h + 1) * hkv] for h in range(MLA_HEADS)],
                          axis=1).T.astype(BF16)
    half = QK_ROPE_DIM // 2
    eye = jnp.eye(half, dtype=F32)
    place = _place_cols(jnp.concatenate([eye, eye], axis=1), [QK_NOPE_DIM], QK_ROPE_DIM, HEAD_PAD)
    base = (jnp.arange(HEAD_PAD) < QK_NOPE_DIM).astype(F32).reshape(1, HEAD_PAD)

    tm = min(TOKEN_TILE, s)
    nt = s // tm
    hv = MLA_HEADS * (V_HEAD_DIM + V_ONES_ROWS)
    const = lambda shape: pl.BlockSpec(shape, lambda i, j: (0,) * len(shape))
    tok = lambda w: pl.BlockSpec((1, tm, w), lambda i, j: (i, j, 0))
    heads = pl.BlockSpec((1, MLA_HEADS, tm, HEAD_PAD), lambda i, j: (i, 0, j, 0))
    dims = (q_rank, kv_rank, ssm_w, d, scale)
    return pl.pallas_call(
        functools.partial(_inproj_body, dims=dims),
        grid=(b, nt),
        in_specs=[tok(d), const((1, d)), const(win.shape), const((1, q_rank)), const((1, kv_rank)),
                  const(wq.shape), const(wqr.shape), const(wk.shape), const(wvt.shape),
                  tok(half), tok(half), const(place.shape), const(base.shape)],
        out_specs=[heads, heads, pl.BlockSpec((1, 1, hv, tm), lambda i, j: (i, j, 0, 0)),
                   tok(ssm_w), tok(d), tok(d)],
        out_shape=[jax.ShapeDtypeStruct((b, MLA_HEADS, s, HEAD_PAD), BF16),
                   jax.ShapeDtypeStruct((b, MLA_HEADS, s, HEAD_PAD), BF16),
                   jax.ShapeDtypeStruct((b, nt, hv, tm), BF16),
                   jax.ShapeDtypeStruct((b, s, ssm_w), BF16),
                   jax.ShapeDtypeStruct((b, s, d), BF16),
                   jax.ShapeDtypeStruct((b, s, d), BF16)],
        compiler_params=_cparams(("parallel", "parallel")),
        name="inproj",
    )(h1, mix_norm.reshape(1, d).astype(F32), win, q_norm.reshape(1, q_rank).astype(F32),
      kv_norm.reshape(1, kv_rank).astype(F32), wq, wqr, wk, wvt, cos_c, sin_c, place, base)


def _attn_body(q_ref, k_ref, vt_ref, o_ref, s_ref, acc_ref, *, n_kt, tk):
    q = q_ref[0, 0]
    tq = q.shape[0]

    def finish():
        acc = acc_ref[...]
        o_ref[0] = (acc[:V_HEAD_DIM] / acc[V_HEAD_DIM:V_HEAD_DIM + 1]).astype(BF16)

    s = _dot_nt(k_ref[0, 0, 0:tk, :], q)
    m = jnp.max(s, axis=0, keepdims=True)
    acc_ref[...] = _dot(vt_ref[0, 0], jnp.exp2(s - m).astype(BF16))
    gap = jnp.zeros_like(m)
    for j in range(1, n_kt):
        s = _dot_nt(k_ref[0, 0, j * tk:(j + 1) * tk, :], q)
        p = jnp.exp2(s - m).astype(BF16)
        m_tile = jnp.max(s, axis=0, keepdims=True)
        gap = jnp.maximum(gap, m_tile - m)
        m_new = jnp.maximum(m, m_tile)
        acc_ref[...] = (acc_ref[...] + _dot(vt_ref[0, j], p)) * jnp.exp2(m - m_new)
        m = m_new
    finish()

    @pl.when(jnp.max(gap) > ATTN_MAX_GAP)
    def _():
        def tile(j, m):
            start = pl.multiple_of(j * tk, tk)
            s_ref[...] = _dot_nt(k_ref[0, 0, pl.ds(start, tk), :], q)
            s = s_ref[...]
            m_new = jnp.maximum(m, jnp.max(s, axis=0, keepdims=True))
            p = jnp.exp2(s - m_new).astype(BF16)
            acc_ref[...] = jnp.exp2(m - m_new) * acc_ref[...] + _dot(vt_ref[0, j], p)
            return m_new

        acc_ref[...] = jnp.zeros_like(acc_ref)
        lax.fori_loop(0, n_kt, tile, jnp.full((1, tq), -jnp.inf, F32))
        finish()


def _attention(q, k, vt):
    b, h, s, _ = q.shape
    n_kt, tk = vt.shape[1], vt.shape[3]
    vrows = vt.shape[2] // h
    tq = min(ATTN_TQ, s)
    return pl.pallas_call(
        functools.partial(_attn_body, n_kt=n_kt, tk=tk),
        grid=(b, h, s // tq),
        in_specs=[pl.BlockSpec((1, 1, tq, HEAD_PAD), lambda i, j, t: (i, j, t, 0)),
                  pl.BlockSpec((1, 1, s, HEAD_PAD), lambda i, j, t: (i, j, 0, 0)),
                  pl.BlockSpec((1, n_kt, vrows, tk), lambda i, j, t: (i, 0, j, 0))],
        out_specs=pl.BlockSpec((1, V_HEAD_DIM, tq), lambda i, j, t: (i, j, t)),
        out_shape=jax.ShapeDtypeStruct((b, h * V_HEAD_DIM, s), BF16),
        scratch_shapes=[pltpu.VMEM((tk, tq), F32), pltpu.VMEM((vrows, tq), F32)],
        compiler_params=_cparams(("parallel", "parallel", "arbitrary")),
        name="attention",
    )(q, k, vt)


def _discretise(lr, li, ldt):
    dt = jnp.exp(ldt)
    mag = jnp.exp(lr * dt)
    lb_re = mag * jnp.cos(li * dt)
    lb_im = mag * jnp.sin(li * dt)
    den = lr * lr + li * li
    nr = lb_re - 1.0
    return (nr * lr + lb_im * li) / den, (lb_im * lr - nr * li) / den, lr * dt, li * dt


def _ssm_prep_body(rowp_ref, colp_ref, bt_re_ref, bt_im_ref, bc_re_ref, bc_im_ref, cr_re_ref, cr_im_ref, dt_ref,
                   tt_ref, pp_ref, qq_ref, dec_ref, *, n_state):
    L = SSM_CHUNK
    rows = tt_ref.shape[1]
    width = rows // L
    lanes2 = 2 * n_state
    parity = pl.program_id(0) % 2
    lane = lax.broadcasted_iota(jnp.int32, (1, lanes2), 1)
    keep_lane = (lane >= parity * n_state) & (lane < (parity + 1) * n_state)
    first_lane = lane < n_state
    krow = (lax.broadcasted_iota(jnp.int32, (rows + width, 1), 0) // width).astype(F32)
    sl = lax.broadcasted_iota(jnp.int32, (1, rows), 1) // width
    blk = lambda a, k: a[k * width:(k + 1) * width]

    k_rows = []
    for d in range(2):
        k_re, k_im, ar, ai = _discretise(rowp_ref[0, d, 0:1, :], rowp_ref[0, d, 1:2, :], rowp_ref[0, d, 2:3, :])
        magk = jnp.exp(krow * ar)
        ak_re = magk * jnp.cos(krow * ai)
        ak_im = magk * jnp.sin(krow * ai)
        bt_re, bt_im = bt_re_ref[0, d], bt_im_ref[0, d]
        bb_re = k_re * bt_re - k_im * bt_im
        bb_im = k_re * bt_im + k_im * bt_re
        ba_re = bb_re * ak_re[:rows] - bb_im * ak_im[:rows]
        ba_im = bb_re * ak_im[:rows] + bb_im * ak_re[:rows]
        blocks = range(L - 1, -1, -1) if d == 0 else range(L)
        p_re = jnp.concatenate([blk(ba_re, k) for k in blocks], axis=0)
        p_im = jnp.concatenate([blk(ba_im, k) for k in blocks], axis=0)
        pp_ref[0, :, (2 * d) * lanes2:(2 * d + 1) * lanes2] = jnp.where(keep_lane, p_re, 0.0).astype(BF16)
        pp_ref[0, :, (2 * d + 1) * lanes2:(2 * d + 2) * lanes2] = jnp.where(keep_lane, p_im, 0.0).astype(BF16)
        cr_re, cr_im = cr_re_ref[0, d], cr_im_ref[0, d]
        ca_re = cr_re * ak_re - cr_im * ak_im
        ca_im = cr_re * ak_im + cr_im * ak_re
        blocks = range(1, L + 1) if d == 0 else range(L, 0, -1)
        q_re = jnp.concatenate([blk(ca_re, k) for k in blocks], axis=0)
        q_im = jnp.concatenate([blk(ca_im, k) for k in blocks], axis=0)
        qq_ref[0, :, (2 * d) * lanes2:(2 * d + 1) * lanes2] = jnp.where(keep_lane, q_re, 0.0).astype(BF16)
        qq_ref[0, :, (2 * d + 1) * lanes2:(2 * d + 2) * lanes2] = jnp.where(keep_lane, -q_im, 0.0).astype(BF16)
        kc_re, kc_im, _, _ = _discretise(colp_ref[0, d, :, 0:1], colp_ref[0, d, :, 1:2], colp_ref[0, d, :, 2:3])
        bc_re, bc_im = bc_re_ref[0, d], bc_im_ref[0, d]
        bbc_re = kc_re * bc_re - kc_im * bc_im
        bbc_im = kc_re * bc_im + kc_im * bc_re
        k_rows.append(_dot_exact(jnp.where(first_lane, ca_re[:rows], 0.0), bbc_re)
                      - _dot_exact(jnp.where(first_lane, ca_im[:rows], 0.0), bbc_im))
        dec_ref[0, 2 * d:2 * d + 1, :] = jnp.where(keep_lane, ak_re[rows:rows + 1], 0.0)
        dec_ref[0, 2 * d + 1:2 * d + 2, :] = jnp.where(keep_lane, ak_im[rows:rows + 1], 0.0)
    dec_ref[0, 4:8, :] = jnp.zeros((4, lanes2), F32)

    kf, kb = k_rows
    diag = blk(kf, 0) + blk(kb, 0) + dt_ref[0]
    for t in range(L):
        acc = jnp.where(sl == t, diag, 0.0)
        for k in range(1, t + 1):
            acc = jnp.where(sl == t - k, blk(kf, k), acc)
        for k in range(1, L - t):
            acc = jnp.where(sl == t + k, blk(kb, k), acc)
        tt_ref[0, t * width:(t + 1) * width, :] = acc.astype(BF16)


def _ssm_prep(lam_re, lam_im, log_dt, b_re, b_im, c_re, c_im, d_skip):
    _, g, p, w = b_re.shape
    L = SSM_CHUNK
    rows = L * w
    dup = lambda a, axis: jnp.concatenate([a, a], axis=axis)
    ldt = jnp.broadcast_to(log_dt[:, :, None], lam_re.shape)
    params = [dup(lam_re, -1), dup(lam_im, -1), dup(ldt, -1)]
    rowp = jnp.pad(jnp.stack(params, axis=2), ((0, 0), (0, 0), (0, SUBLANES - 3), (0, 0)))
    colp = jnp.pad(jnp.stack(params, axis=3), ((0, 0), (0, 0), (0, 0), (0, LANES - 3)))
    per_group = lambda a: jnp.swapaxes(a, 0, 1).astype(F32)
    bt = lambda b: jnp.tile(dup(jnp.swapaxes(b, 2, 3), -1), (1, 1, L, 1))
    bc = lambda b: jnp.tile(dup(b, 2), (1, 1, 1, L))
    cr = lambda c: jnp.tile(dup(c, -1), (1, 1, L + 1, 1))
    dmat = jnp.where(jnp.eye(w, dtype=bool)[None], d_skip.reshape(g, 1, w), 0.0)
    dtile = jnp.tile(dmat, (1, 1, L)).astype(F32)
    grp = lambda shape: pl.BlockSpec((1,) + shape, lambda i: (i,) + (0,) * len(shape))
    return pl.pallas_call(
        functools.partial(_ssm_prep_body, n_state=p),
        grid=(g,),
        in_specs=[grp((2, SUBLANES, 2 * p)), grp((2, 2 * p, LANES)), grp((2, rows, 2 * p)), grp((2, rows, 2 * p)),
                  grp((2, 2 * p, rows)), grp((2, 2 * p, rows)), grp((2, rows + w, 2 * p)), grp((2, rows + w, 2 * p)),
                  grp((w, rows))],
        out_specs=[grp((rows, rows)), grp((rows, 8 * p)), grp((rows, 8 * p)), grp((SUBLANES, 2 * p))],
        out_shape=[jax.ShapeDtypeStruct((g, rows, rows), BF16),
                   jax.ShapeDtypeStruct((g, rows, 8 * p), BF16),
                   jax.ShapeDtypeStruct((g, rows, 8 * p), BF16),
                   jax.ShapeDtypeStruct((g, SUBLANES, 2 * p), F32)],
        compiler_params=_cparams(("parallel",)),
        name="ssm_prep",
    )(per_group(rowp), per_group(colp), per_group(bt(b_re)), per_group(bt(b_im)), per_group(bc(b_re)),
      per_group(bc(b_im)), per_group(cr(c_re)), per_group(cr(c_im)), dtile)


def _utile_body(u_ref, eye_ref, o_ref, scr_ref):
    scr_ref[...] = u_ref[0].astype(F32)
    half = SSM_CHUNK // 2
    for sh in range(2):
        rows = jnp.concatenate([scr_ref[:, sh * half + s, :] for s in range(half)], axis=0).astype(BF16)
        o_ref[0, sh] = _dot_nt(eye_ref[...], rows).astype(BF16)


def _utile(u):
    b, s, w = u.shape
    L = SSM_CHUNK
    n_cb = s // (L * LANES)
    u4 = u.reshape(b, s // L, L, w)
    eye = jnp.eye(w, dtype=BF16)
    return pl.pallas_call(
        _utile_body,
        grid=(b, n_cb),
        in_specs=[pl.BlockSpec((1, LANES, L, w), lambda i, j: (i, j, 0, 0)),
                  pl.BlockSpec((w, w), lambda i, j: (0, 0))],
        out_specs=pl.BlockSpec((1, 2, w, LANES * L // 2), lambda i, j: (i, j, 0, 0)),
        out_shape=jax.ShapeDtypeStruct((b, 2 * n_cb, w, LANES * L // 2), BF16),
        scratch_shapes=[pltpu.VMEM((LANES, L, w), F32)],
        compiler_params=_cparams(("parallel", "parallel")),
        name="ssm_utile",
    )(u4, eye)


def _ssm_body(u_ref, tt_ref, pp_ref, qq_ref, dec_ref, y_ref, st_ref, *, n_chunks):
    bsz = u_ref.shape[0]
    n_cb = u_ref.shape[1] // 2
    width = u_ref.shape[2] // 2
    L = SSM_CHUNK
    half = L // 2
    w = LANES
    assert bsz == SUBLANES

    def tile_of(cb, step):
        return 2 * cb + step // half, slice((step % half) * w, (step % half + 1) * w)

    def ucol(b, g):
        pieces = []
        for s in range(L):
            row = []
            for cb in range(n_cb):
                tile, lanes = tile_of(cb, s)
                row.append(u_ref[b, tile, g * width:(g + 1) * width, lanes])
            pieces.append(jnp.concatenate(row, axis=1))
        return jnp.concatenate(pieces, axis=0)

    for b in range(bsz):
        e = _dot_tn(ucol(b, 0), pp_ref[0]) + _dot_tn(ucol(b, 1), pp_ref[1])
        for i in range(4):
            st_ref[i, pl.ds(b, n_chunks, stride=bsz), :] = e[:, i * w:(i + 1) * w]

    dec = dec_ref[0] + dec_ref[1]
    a_fr, a_fi, a_br, a_bi = (jnp.broadcast_to(dec[i:i + 1, :], (bsz, w)) for i in range(4))

    def step(c, carry):
        hfr, hfi, hbr, hbi = carry
        rf = pl.multiple_of(c * bsz, bsz)
        rb = pl.multiple_of((n_chunks - 1 - c) * bsz, bsz)
        efr = st_ref[0, pl.ds(rf, bsz), :]
        efi = st_ref[1, pl.ds(rf, bsz), :]
        ebr = st_ref[2, pl.ds(rb, bsz), :]
        ebi = st_ref[3, pl.ds(rb, bsz), :]
        st_ref[0, pl.ds(rf, bsz), :] = hfr
        st_ref[1, pl.ds(rf, bsz), :] = hfi
        st_ref[2, pl.ds(rb, bsz), :] = hbr
        st_ref[3, pl.ds(rb, bsz), :] = hbi
        return (a_fr * hfr - a_fi * hfi + efr, a_fi * hfr + a_fr * hfi + efi,
                a_br * hbr - a_bi * hbi + ebr, a_bi * hbr + a_br * hbi + ebi)

    zero = jnp.zeros((bsz, w), F32)
    lax.fori_loop(0, n_chunks, step, (zero, zero, zero, zero))

    for b in range(bsz):
        h = jnp.concatenate([st_ref[i, pl.ds(b, n_chunks, stride=bsz), :] for i in range(4)],
                            axis=1).astype(BF16)
        for g in range(2):
            y = (_dot(tt_ref[g], ucol(b, g)) + _dot_nt(qq_ref[g], h)).astype(BF16)
            for t in range(L):
                for cb in range(n_cb):
                    tile, lanes = tile_of(cb, t)
                    y_ref[b, tile, g * width:(g + 1) * width, lanes] = y[t * width:(t + 1) * width, cb * w:(cb + 1) * w]


def _ssm(u_t, tt, pp, qq, dec):
    b, n_tiles, ssm_w, tl = u_t.shape
    g, k, _ = tt.shape
    n_state8 = pp.shape[2]
    width = k // SSM_CHUNK
    n_chunks = n_tiles // 2 * LANES
    pair = lambda shape: pl.BlockSpec((2,) + shape, lambda i: (i,) + (0,) * len(shape))
    seq = pl.BlockSpec((b, n_tiles, 2 * width, tl), lambda i: (0, 0, i, 0))
    return pl.pallas_call(
        functools.partial(_ssm_body, n_chunks=n_chunks),
        grid=(g // 2,),
        in_specs=[seq, pair((k, k)), pair((k, n_state8)), pair((k, n_state8)), pair((SUBLANES, dec.shape[2]))],
        out_specs=seq,
        out_shape=jax.ShapeDtypeStruct(u_t.shape, BF16),
        scratch_shapes=[pltpu.VMEM((n_state8 // LANES, n_chunks * b, LANES), F32)],
        compiler_params=_cparams(("parallel",)),
        name="ssm",
    )(u_t, tt, pp, qq, dec)


def _ssm_post_body(y_ref, wglu_ref, bglu_ref, wos_ref, o_ref):
    ssm_w = y_ref.shape[2]
    y = y_ref[0, 0].astype(F32)
    glu = _dot_tn(jax.nn.gelu(y).astype(BF16), wglu_ref[...]) + bglu_ref[...]
    ssm_out = glu[:, :ssm_w] * jax.nn.sigmoid(glu[:, ssm_w:])
    out = _dot(ssm_out.astype(BF16), wos_ref[...])
    for s in range(SSM_CHUNK // 2):
        o_ref[0, :, s, :] = out[s * LANES:(s + 1) * LANES]


def _ssm_post(y_t, w_glu, b_glu, w_o_ssm):
    b, n_tiles, ssm_w, tl = y_t.shape
    d = w_o_ssm.shape[1]
    L = SSM_CHUNK
    s = n_tiles * tl
    const = lambda shape: pl.BlockSpec(shape, lambda i, j: (0,) * len(shape))
    out = pl.pallas_call(
        _ssm_post_body,
        grid=(b, n_tiles),
        in_specs=[pl.BlockSpec((1, 1, ssm_w, tl), lambda i, j: (i, j, 0, 0)),
                  const(w_glu.shape), const((1, w_glu.shape[1])), const(w_o_ssm.shape)],
        out_specs=pl.BlockSpec((1, LANES, L // 2, d), lambda i, j: (i, j // 2, j % 2, 0)),
        out_shape=jax.ShapeDtypeStruct((b, s // L, L, d), F32),
        compiler_params=_cparams(("parallel", "parallel")),
        name="ssm_post",
    )(y_t, w_glu.astype(BF16), b_glu.reshape(1, -1).astype(F32), w_o_ssm.astype(BF16))
    return out.reshape(b, s, d)


def _merge_ffn_body(h_ref, at_ref, ssm_ref, ga_ref, gs_ref, woa_ref, wout_ref, g_ref, wg_ref, wu_ref, wd_ref,
                    *rest):
    o_ref = rest[-1]
    attn = _dot_tn(at_ref[0], woa_ref[...])
    merged = ga_ref[0].astype(F32) * attn + gs_ref[0].astype(F32) * ssm_ref[0]
    h = h_ref[0] + _dot(merged.astype(BF16), wout_ref[...])
    h = _ffn_apply(h, g_ref, wg_ref, wu_ref, wd_ref)
    if len(rest) == 2:
        h = _rms(h, rest[0][...])
    o_ref[0] = h


def _merge_ffn(h1, attn_t, ssm, ga, gs, w_o_attn, w_out, norm, w_gate, w_up, w_down, final_norm=None):
    b, s, d = h1.shape
    hv = attn_t.shape[1]
    tm = min(TOKEN_TILE, s)
    tok = lambda w: pl.BlockSpec((1, tm, w), lambda i, j: (i, j, 0))
    ffn_args, ffn_specs = _ffn_operands(norm, w_gate, w_up, w_down)
    args = [h1, attn_t, ssm, ga, gs, w_o_attn.astype(BF16), w_out.astype(BF16)] + ffn_args
    in_specs = [tok(d), pl.BlockSpec((1, hv, tm), lambda i, j: (i, 0, j)), tok(d), tok(d), tok(d),
                _resident(w_o_attn.shape), _resident(w_out.shape)] + ffn_specs
    if final_norm is not None:
        args.append(final_norm.reshape(1, d).astype(F32))
        in_specs.append(_resident((1, d)))
    return pl.pallas_call(
        _merge_ffn_body,
        grid=(b, s // tm),
        in_specs=in_specs,
        out_specs=tok(d),
        out_shape=jax.ShapeDtypeStruct((b, s, d), F32),
        compiler_params=_cparams(("parallel", "parallel")),
        name="merge_ffn",
    )(*args)


def kernel(x, positions, ffn1_norm, ffn1_w_gate, ffn1_w_up, ffn1_w_down, mix_norm, w_in, q_norm, w_uq, kv_norm, w_ukv, w_o_attn, ssm_lambda_re, ssm_lambda_im, ssm_log_dt, ssm_b_re, ssm_b_im, ssm_c_re, ssm_c_im, ssm_d, w_glu, b_glu, w_o_ssm, w_out, ffn2_norm, ffn2_w_gate, ffn2_w_up, ffn2_w_down, final_norm):
    b, s, d = x.shape
    depth = ffn1_norm.shape[0]
    _, _, n_grp, _, grp_w = ssm_b_re.shape
    ssm_w = n_grp * grp_w
    L = SSM_CHUNK
    assert depth >= 1, "the final norm is fused into the last layer's second FFN"
    assert b == SUBLANES, "the S5 chunk recurrence keeps the batch on the 8 sublanes"
    assert s % max(TOKEN_TILE, ATTN_TQ, L * LANES) == 0 and L * grp_w == MXU_DIM and n_grp % 2 == 0

    cos_c, sin_c = _rope_tables(positions)
    h = x
    for l in range(depth):
        last = l == depth - 1
        h = _ffn(h.reshape(b * s, d), ffn1_norm[l], ffn1_w_gate[l], ffn1_w_up[l], ffn1_w_down[l]).reshape(b, s, d)
        q, k, vt, u, ga, gs = _inproj(h, cos_c, sin_c, mix_norm[l], w_in[l], q_norm[l], w_uq[l],
                                      kv_norm[l], w_ukv[l], ssm_w)
        attn_t = _attention(q, k, vt)

        tt, pp, qq, dec = _ssm_prep(ssm_lambda_re[l], ssm_lambda_im[l], ssm_log_dt[l], ssm_b_re[l], ssm_b_im[l],
                                    ssm_c_re[l], ssm_c_im[l], ssm_d[l])
        y_t = _ssm(_utile(u), tt, pp, qq, dec)
        ssm = _ssm_post(y_t, w_glu[l], b_glu[l], w_o_ssm[l])

        h = _merge_ffn(h, attn_t, ssm, ga, gs, w_o_attn[l], w_out[l], ffn2_norm[l], ffn2_w_gate[l],
                       ffn2_w_up[l], ffn2_w_down[l], final_norm if last else None)
    return h
```

```python
import functools
import math

import jax
import jax.numpy as jnp
from jax import lax
from jax.experimental import pallas as pl
from jax.experimental.pallas import tpu as pltpu

F32 = jnp.float32
BF16 = jnp.bfloat16

MLA_HEADS = 8
QK_NOPE_DIM = 64
QK_ROPE_DIM = 32
V_HEAD_DIM = 64
ROPE_THETA = 10000.0
FFN_RES = 0.5
EPS = 1e-6

LANES = 128
SUBLANES = 8
MXU_DIM = 256
VMEM_LIMIT_BYTES = 56 * 1024 * 1024

HEAD_PAD = LANES
V_ONES_ROWS = 16
SSM_CHUNK = 16

TOKEN_TILE = 512
ATTN_TQ = 1024
ATTN_TK = 512
FF_CHUNK = 256
ATTN_MAX_GAP = 64.0

def _cparams(sem, flags=None):
    return pltpu.CompilerParams(dimension_semantics=sem, vmem_limit_bytes=VMEM_LIMIT_BYTES, flags=flags)


def _rms(x, g):
    return x * lax.rsqrt(jnp.mean(x * x, axis=-1, keepdims=True) + EPS) * g


def _dot(a, b):
    return jnp.dot(a, b, preferred_element_type=F32)


def _dot_nt(a, b):
    return lax.dot_general(a, b, (((1,), (1,)), ((), ())), preferred_element_type=F32)


def _dot_tn(a, b):
    return lax.dot_general(a, b, (((0,), (0,)), ((), ())), preferred_element_type=F32)


def _dot_exact(a, b):
    return jnp.dot(a, b, preferred_element_type=F32, precision=lax.Precision.HIGHEST)


def _dot_exact_tn(a, b):
    return lax.dot_general(a, b, (((0,), (0,)), ((), ())), preferred_element_type=F32,
                           precision=lax.Precision.HIGHEST)


def _rope_body(pos_ref, invf_ref, cos_ref, sin_ref):
    ang = invf_ref[...] * pos_ref[0]
    cos_ref[0] = jnp.cos(ang)
    sin_ref[0] = jnp.sin(ang)


def _rope_tables(positions):
    b, s = positions.shape
    half = QK_ROPE_DIM // 2
    inv_freq = ROPE_THETA ** (-jnp.arange(0, QK_ROPE_DIM, 2, dtype=F32) / QK_ROPE_DIM)
    out = pl.BlockSpec((1, half, s), lambda i: (i, 0, 0))
    return pl.pallas_call(
        _rope_body,
        grid=(b,),
        in_specs=[pl.BlockSpec((1, 1, s), lambda i: (i, 0, 0)), pl.BlockSpec((half, 1), lambda i: (0, 0))],
        out_specs=[out, out],
        out_shape=[jax.ShapeDtypeStruct((b, half, s), F32)] * 2,
        compiler_params=_cparams(("parallel",)),
        name="rope_tables",
    )(positions.astype(F32).reshape(b, 1, s), inv_freq.reshape(half, 1))


def _ffn_apply(x, g_ref, wg_ref, wu_ref, wd_ref):
    n = _rms(x, g_ref[...]).astype(BF16)
    acc = jnp.zeros_like(x)
    for j in range(wg_ref.shape[1] // FF_CHUNK):
        cols = slice(j * FF_CHUNK, (j + 1) * FF_CHUNK)
        gate = _dot(n, wg_ref[:, cols])
        up = _dot(n, wu_ref[:, cols])
        act = (gate * jax.nn.sigmoid(gate) * up).astype(BF16)
        acc = acc + _dot(act, wd_ref[cols, :])
    return x + FFN_RES * acc


def _ffn_body(x_ref, g_ref, wg_ref, wu_ref, wd_ref, o_ref):
    o_ref[...] = _ffn_apply(x_ref[...], g_ref, wg_ref, wu_ref, wd_ref)


def _resident(shape):
    return pl.BlockSpec(shape, lambda *_: (0,) * len(shape), pipeline_mode=pl.Buffered(1))


def _ffn_operands(norm, w_gate, w_up, w_down):
    d, d_ff = w_gate.shape
    assert d_ff % FF_CHUNK == 0
    args = [norm.reshape(1, d).astype(F32), w_gate.astype(BF16), w_up.astype(BF16), w_down.astype(BF16)]
    return args, [_resident(a.shape) for a in args]


def _ffn(x2, norm, w_gate, w_up, w_down):
    t, d = x2.shape
    tm = min(TOKEN_TILE, t)
    row = pl.BlockSpec((tm, d), lambda i: (i, 0))
    args, specs = _ffn_operands(norm, w_gate, w_up, w_down)
    return pl.pallas_call(
        _ffn_body,
        grid=(t // tm,),
        in_specs=[row] + specs,
        out_specs=row,
        out_shape=jax.ShapeDtypeStruct((t, d), F32),
        compiler_params=_cparams(("parallel",)),
        name="ffn",
    )(x2, *args)


def _inproj_body(h_ref, g_ref, win_ref, qn_ref, kvn_ref, wq_ref, wqr_ref, wk_ref, wvt_ref,
                 cos_ref, sin_ref, place_ref, base_ref,
                 q_ref, k_ref, vt_ref, u_ref, ga_ref, gs_ref, *, dims):
    q_rank, kv_rank, ssm_w, d_model, scale = dims
    n = _rms(h_ref[0], g_ref[...]).astype(BF16)
    z = _dot(n, win_ref[...])
    o = 0
    c_q = z[:, o:o + q_rank]; o += q_rank
    c_kv = z[:, o:o + kv_rank]; o += kv_rank
    kr = z[:, o:o + HEAD_PAD]; o += HEAD_PAD
    kr_rot = z[:, o:o + HEAD_PAD]; o += HEAD_PAD
    u = z[:, o:o + ssm_w]; o += ssm_w
    g_attn = z[:, o:o + d_model]; o += d_model
    g_ssm = z[:, o:o + d_model]

    u_ref[0] = u.astype(BF16)
    ga_ref[0] = jax.nn.sigmoid(g_attn).astype(BF16)
    gs_ref[0] = jax.nn.sigmoid(g_ssm).astype(BF16)

    cos = _dot_exact_tn(cos_ref[0], place_ref[...]) + base_ref[...]
    sin = _dot_exact_tn(sin_ref[0], place_ref[...])

    cqn = _rms(c_q, qn_ref[...]).astype(BF16)
    q_all = _dot(cqn, wq_ref[...])
    q_rot = _dot(cqn, wqr_ref[...])
    ckvn = _rms(c_kv, kvn_ref[...]).astype(BF16)
    k_nope = _dot(ckvn, wk_ref[...])
    k_rope = kr * cos + kr_rot * sin
    for h in range(MLA_HEADS):
        sl = slice(h * HEAD_PAD, (h + 1) * HEAD_PAD)
        q_ref[0, h] = ((q_all[:, sl] * cos + q_rot[:, sl] * sin) * scale).astype(BF16)
        k_ref[0, h] = (k_nope[:, sl] + k_rope).astype(BF16)
    vt = _dot_nt(wvt_ref[...], ckvn).astype(BF16)
    ones = jnp.ones((V_ONES_ROWS, vt.shape[1]), BF16)
    vt_ref[0, 0] = jnp.concatenate(
        [piece for h in range(MLA_HEADS) for piece in (vt[h * V_HEAD_DIM:(h + 1) * V_HEAD_DIM], ones)], axis=0)


def _place_cols(w, starts, width, total):
    out = jnp.zeros((w.shape[0], total), w.dtype)
    for i, s in enumerate(starts):
        out = lax.dynamic_update_slice(out, w[:, i * width:(i + 1) * width], (0, s))
    return out


def _rot_half(w):
    half = w.shape[1] // 2
    return jnp.concatenate([-w[:, half:], w[:, :half]], axis=1)


def _inproj(h1, cos_c, sin_c, mix_norm, w_in, q_norm, w_uq, kv_norm, w_ukv, ssm_w):
    b, s, d = h1.shape
    q_rank, kv_rank = q_norm.shape[0], kv_norm.shape[0]
    hq = QK_NOPE_DIM + QK_ROPE_DIM
    hkv = QK_NOPE_DIM + V_HEAD_DIM
    scale = float(hq) ** -0.5 * math.log2(math.e)
    o_kr = q_rank + kv_rank
    o_u = o_kr + QK_ROPE_DIM
    w_kr = w_in[:, o_kr:o_u]
    kr_p = _place_cols(w_kr, [QK_NOPE_DIM], QK_ROPE_DIM, HEAD_PAD)
    kr_rot_p = _place_cols(_rot_half(w_kr), [QK_NOPE_DIM], QK_ROPE_DIM, HEAD_PAD)
    win = jnp.concatenate([w_in[:, :o_kr], kr_p, kr_rot_p, w_in[:, o_u:]], axis=1).astype(BF16)
    wq = jnp.concatenate([jnp.pad(w_uq[:, h * hq:(h + 1) * hq], ((0, 0), (0, HEAD_PAD - hq)))
                          for h in range(MLA_HEADS)], axis=1).astype(BF16)
    wqr = jnp.concatenate(
        [_place_cols(_rot_half(w_uq[:, h * hq + QK_NOPE_DIM:(h + 1) * hq]), [QK_NOPE_DIM], QK_ROPE_DIM, HEAD_PAD)
         for h in range(MLA_HEADS)], axis=1).astype(BF16)
    wk = jnp.concatenate([jnp.pad(w_ukv[:, h * hkv:h * hkv + QK_NOPE_DIM], ((0, 0), (0, HEAD_PAD - QK_NOPE_DIM)))
                          for h in range(MLA_HEADS)], axis=1).astype(BF16)
    wvt = jnp.concatenate([w_ukv[:, h * hkv + QK_NOPE_DIM:(h + 1) * hkv] for h in range(MLA_HEADS)],
                          axis=1).T.astype(BF16)
    half = QK_ROPE_DIM // 2
    eye = jnp.eye(half, dtype=F32)
    place = _place_cols(jnp.concatenate([eye, eye], axis=1), [QK_NOPE_DIM], QK_ROPE_DIM, HEAD_PAD)
    base = (jnp.arange(HEAD_PAD) < QK_NOPE_DIM).astype(F32).reshape(1, HEAD_PAD)

    tm = min(TOKEN_TILE, s)
    nt = s // tm
    hv = MLA_HEADS * (V_HEAD_DIM + V_ONES_ROWS)
    const = lambda shape: pl.BlockSpec(shape, lambda i, j: (0,) * len(shape))
    tok = lambda w: pl.BlockSpec((1, tm, w), lambda i, j: (i, j, 0))
    heads = pl.BlockSpec((1, MLA_HEADS, tm, HEAD_PAD), lambda i, j: (i, 0, j, 0))
    rope = pl.BlockSpec((1, half, tm), lambda i, j: (i, 0, j))
    dims = (q_rank, kv_rank, ssm_w, d, scale)
    return pl.pallas_call(
        functools.partial(_inproj_body, dims=dims),
        grid=(b, nt),
        in_specs=[tok(d), const((1, d)), const(win.shape), const((1, q_rank)), const((1, kv_rank)),
                  const(wq.shape), const(wqr.shape), const(wk.shape), const(wvt.shape),
                  rope, rope, const(place.shape), const(base.shape)],
        out_specs=[heads, heads, pl.BlockSpec((1, 1, hv, tm), lambda i, j: (i, j, 0, 0)),
                   tok(ssm_w), tok(d), tok(d)],
        out_shape=[jax.ShapeDtypeStruct((b, MLA_HEADS, s, HEAD_PAD), BF16),
                   jax.ShapeDtypeStruct((b, MLA_HEADS, s, HEAD_PAD), BF16),
                   jax.ShapeDtypeStruct((b, nt, hv, tm), BF16),
                   jax.ShapeDtypeStruct((b, s, ssm_w), BF16),
                   jax.ShapeDtypeStruct((b, s, d), BF16),
                   jax.ShapeDtypeStruct((b, s, d), BF16)],
        compiler_params=_cparams(("parallel", "parallel")),
        name="inproj",
    )(h1, mix_norm.reshape(1, d).astype(F32), win, q_norm.reshape(1, q_rank).astype(F32),
      kv_norm.reshape(1, kv_rank).astype(F32), wq, wqr, wk, wvt, cos_c, sin_c, place, base)


def _attn_body(q_ref, k_ref, vt_ref, o_ref, s_ref, acc_ref, *, n_kt, tk):
    q = q_ref[0, 0]
    tq = q.shape[0]

    def finish():
        acc = acc_ref[...]
        o_ref[0] = (acc[:V_HEAD_DIM] / acc[V_HEAD_DIM:V_HEAD_DIM + 1]).astype(BF16)

    s = _dot_nt(k_ref[0, 0, 0:tk, :], q)
    m = jnp.max(s, axis=0, keepdims=True)
    acc_ref[...] = _dot(vt_ref[0, 0], jnp.exp2(s - m).astype(BF16))
    gap = jnp.zeros_like(m)
    for j in range(1, n_kt):
        s = _dot_nt(k_ref[0, 0, j * tk:(j + 1) * tk, :], q)
        p = jnp.exp2(s - m).astype(BF16)
        m_tile = jnp.max(s, axis=0, keepdims=True)
        gap = jnp.maximum(gap, m_tile - m)
        m_new = jnp.maximum(m, m_tile)
        acc_ref[...] = (acc_ref[...] + _dot(vt_ref[0, j], p)) * jnp.exp2(m - m_new)
        m = m_new
    finish()

    @pl.when(jnp.max(gap) > ATTN_MAX_GAP)
    def _():
        def tile(j, m):
            start = pl.multiple_of(j * tk, tk)
            s_ref[...] = _dot_nt(k_ref[0, 0, pl.ds(start, tk), :], q)
            s = s_ref[...]
            m_new = jnp.maximum(m, jnp.max(s, axis=0, keepdims=True))
            p = jnp.exp2(s - m_new).astype(BF16)
            acc_ref[...] = jnp.exp2(m - m_new) * acc_ref[...] + _dot(vt_ref[0, j], p)
            return m_new

        acc_ref[...] = jnp.zeros_like(acc_ref)
        lax.fori_loop(0, n_kt, tile, jnp.full((1, tq), -jnp.inf, F32))
        finish()


def _attention(q, k, vt):
    b, h, s, _ = q.shape
    n_kt, tk = vt.shape[1], vt.shape[3]
    vrows = vt.shape[2] // h
    tq = min(ATTN_TQ, s)
    return pl.pallas_call(
        functools.partial(_attn_body, n_kt=n_kt, tk=tk),
        grid=(b, h, s // tq),
        in_specs=[pl.BlockSpec((1, 1, tq, HEAD_PAD), lambda i, j, t: (i, j, t, 0)),
                  pl.BlockSpec((1, 1, s, HEAD_PAD), lambda i, j, t: (i, j, 0, 0)),
                  pl.BlockSpec((1, n_kt, vrows, tk), lambda i, j, t: (i, 0, j, 0))],
        out_specs=pl.BlockSpec((1, V_HEAD_DIM, tq), lambda i, j, t: (i, j, t)),
        out_shape=jax.ShapeDtypeStruct((b, h * V_HEAD_DIM, s), BF16),
        scratch_shapes=[pltpu.VMEM((tk, tq), F32), pltpu.VMEM((vrows, tq), F32)],
        compiler_params=_cparams(("parallel", "parallel", "arbitrary")),
        name="attention",
    )(q, k, vt)


def _discretise(lr, li, ldt):
    dt = jnp.exp(ldt)
    mag = jnp.exp(lr * dt)
    lb_re = mag * jnp.cos(li * dt)
    lb_im = mag * jnp.sin(li * dt)
    den = lr * lr + li * li
    nr = lb_re - 1.0
    return (nr * lr + lb_im * li) / den, (lb_im * lr - nr * li) / den, lr * dt, li * dt


def _ssm_prep_body(rowp_ref, colp_ref, bt_re_ref, bt_im_ref, bc_re_ref, bc_im_ref, cr_re_ref, cr_im_ref, dt_ref,
                   tt_ref, pp_ref, qq_ref, dec_ref, *, n_state):
    L = SSM_CHUNK
    rows = tt_ref.shape[1]
    width = rows // L
    lanes2 = 2 * n_state
    parity = pl.program_id(0) % 2
    lane = lax.broadcasted_iota(jnp.int32, (1, lanes2), 1)
    keep_lane = (lane >= parity * n_state) & (lane < (parity + 1) * n_state)
    first_lane = lane < n_state
    n_pow = (L + 1 + SUBLANES - 1) // SUBLANES * SUBLANES
    kpow = lax.broadcasted_iota(jnp.int32, (n_pow, 1), 0).astype(F32)
    rep = (lax.broadcasted_iota(jnp.int32, (rows + width, n_pow), 0) // width
           == lax.broadcasted_iota(jnp.int32, (rows + width, n_pow), 1)).astype(F32)
    sl = lax.broadcasted_iota(jnp.int32, (1, rows), 1) // width
    blk = lambda a, k: a[k * width:(k + 1) * width]

    k_rows = []
    for d in range(2):
        k_re, k_im, ar, ai = _discretise(rowp_ref[0, d, 0:1, :], rowp_ref[0, d, 1:2, :], rowp_ref[0, d, 2:3, :])
        magk = jnp.exp(kpow * ar)
        ak_re = _dot_exact(rep, magk * jnp.cos(kpow * ai))
        ak_im = _dot_exact(rep, magk * jnp.sin(kpow * ai))
        bt_re, bt_im = bt_re_ref[0, d], bt_im_ref[0, d]
        bb_re = k_re * bt_re - k_im * bt_im
        bb_im = k_re * bt_im + k_im * bt_re
        ba_re = bb_re * ak_re[:rows] - bb_im * ak_im[:rows]
        ba_im = bb_re * ak_im[:rows] + bb_im * ak_re[:rows]
        blocks = range(L - 1, -1, -1) if d == 0 else range(L)
        p_re = jnp.concatenate([blk(ba_re, k) for k in blocks], axis=0)
        p_im = jnp.concatenate([blk(ba_im, k) for k in blocks], axis=0)
        pp_ref[0, :, (2 * d) * lanes2:(2 * d + 1) * lanes2] = jnp.where(keep_lane, p_re, 0.0).astype(BF16)
        pp_ref[0, :, (2 * d + 1) * lanes2:(2 * d + 2) * lanes2] = jnp.where(keep_lane, p_im, 0.0).astype(BF16)
        cr_re, cr_im = cr_re_ref[0, d], cr_im_ref[0, d]
        ca_re = cr_re * ak_re - cr_im * ak_im
        ca_im = cr_re * ak_im + cr_im * ak_re
        blocks = range(1, L + 1) if d == 0 else range(L, 0, -1)
        q_re = jnp.concatenate([blk(ca_re, k) for k in blocks], axis=0)
        q_im = jnp.concatenate([blk(ca_im, k) for k in blocks], axis=0)
        qq_ref[0, :, (2 * d) * lanes2:(2 * d + 1) * lanes2] = jnp.where(keep_lane, q_re, 0.0).astype(BF16)
        qq_ref[0, :, (2 * d + 1) * lanes2:(2 * d + 2) * lanes2] = jnp.where(keep_lane, -q_im, 0.0).astype(BF16)
        kc_re, kc_im, _, _ = _discretise(colp_ref[0, d, :, 0:1], colp_ref[0, d, :, 1:2], colp_ref[0, d, :, 2:3])
        bc_re, bc_im = bc_re_ref[0, d], bc_im_ref[0, d]
        bbc_re = kc_re * bc_re - kc_im * bc_im
        bbc_im = kc_re * bc_im + kc_im * bc_re
        k_rows.append(_dot_exact(jnp.where(first_lane, ca_re[:rows], 0.0), bbc_re)
                      - _dot_exact(jnp.where(first_lane, ca_im[:rows], 0.0), bbc_im))
        dec_ref[0, 2 * d:2 * d + 1, :] = jnp.where(keep_lane, ak_re[rows:rows + 1], 0.0)
        dec_ref[0, 2 * d + 1:2 * d + 2, :] = jnp.where(keep_lane, ak_im[rows:rows + 1], 0.0)
    dec_ref[0, 4:8, :] = jnp.zeros((4, lanes2), F32)

    kf, kb = k_rows
    diag = blk(kf, 0) + blk(kb, 0) + dt_ref[0]
    for t in range(L):
        acc = jnp.where(sl == t, diag, 0.0)
        for k in range(1, t + 1):
            acc = jnp.where(sl == t - k, blk(kf, k), acc)
        for k in range(1, L - t):
            acc = jnp.where(sl == t + k, blk(kb, k), acc)
        tt_ref[0, t * width:(t + 1) * width, :] = acc.astype(BF16)


def _ssm_prep(lam_re, lam_im, log_dt, b_re, b_im, c_re, c_im, d_skip):
    _, g, p, w = b_re.shape
    L = SSM_CHUNK
    rows = L * w
    dup = lambda a, axis: jnp.concatenate([a, a], axis=axis)
    ldt = jnp.broadcast_to(log_dt[:, :, None], lam_re.shape)
    params = [dup(lam_re, -1), dup(lam_im, -1), dup(ldt, -1)]
    rowp = jnp.pad(jnp.stack(params, axis=2), ((0, 0), (0, 0), (0, SUBLANES - 3), (0, 0)))
    colp = jnp.pad(jnp.stack(params, axis=3), ((0, 0), (0, 0), (0, 0), (0, LANES - 3)))
    per_group = lambda a: jnp.swapaxes(a, 0, 1).astype(F32)
    bt = lambda b: jnp.tile(dup(jnp.swapaxes(b, 2, 3), -1), (1, 1, L, 1))
    bc = lambda b: jnp.tile(dup(b, 2), (1, 1, 1, L))
    cr = lambda c: jnp.tile(dup(c, -1), (1, 1, L + 1, 1))
    dmat = jnp.where(jnp.eye(w, dtype=bool)[None], d_skip.reshape(g, 1, w), 0.0)
    dtile = jnp.tile(dmat, (1, 1, L)).astype(F32)
    grp = lambda shape: pl.BlockSpec((1,) + shape, lambda i: (i,) + (0,) * len(shape))
    return pl.pallas_call(
        functools.partial(_ssm_prep_body, n_state=p),
        grid=(g,),
        in_specs=[grp((2, SUBLANES, 2 * p)), grp((2, 2 * p, LANES)), grp((2, rows, 2 * p)), grp((2, rows, 2 * p)),
                  grp((2, 2 * p, rows)), grp((2, 2 * p, rows)), grp((2, rows + w, 2 * p)), grp((2, rows + w, 2 * p)),
                  grp((w, rows))],
        out_specs=[grp((rows, rows)), grp((rows, 8 * p)), grp((rows, 8 * p)), grp((SUBLANES, 2 * p))],
        out_shape=[jax.ShapeDtypeStruct((g, rows, rows), BF16),
                   jax.ShapeDtypeStruct((g, rows, 8 * p), BF16),
                   jax.ShapeDtypeStruct((g, rows, 8 * p), BF16),
                   jax.ShapeDtypeStruct((g, SUBLANES, 2 * p), F32)],
        compiler_params=_cparams(("parallel",)),
        name="ssm_prep",
    )(per_group(rowp), per_group(colp), per_group(bt(b_re)), per_group(bt(b_im)), per_group(bc(b_re)),
      per_group(bc(b_im)), per_group(cr(c_re)), per_group(cr(c_im)), dtile)


def _utile_body(u_ref, eye_ref, o_ref, scr_ref):
    scr_ref[...] = u_ref[0].astype(F32)
    half = SSM_CHUNK // 2
    for sh in range(2):
        rows = jnp.concatenate([scr_ref[:, sh * half + s, :] for s in range(half)], axis=0).astype(BF16)
        o_ref[0, sh] = _dot_nt(eye_ref[...], rows).astype(BF16)


def _utile(u):
    b, s, w = u.shape
    L = SSM_CHUNK
    n_cb = s // (L * LANES)
    u4 = u.reshape(b, s // L, L, w)
    eye = jnp.eye(w, dtype=BF16)
    return pl.pallas_call(
        _utile_body,
        grid=(b, n_cb),
        in_specs=[pl.BlockSpec((1, LANES, L, w), lambda i, j: (i, j, 0, 0)),
                  pl.BlockSpec((w, w), lambda i, j: (0, 0))],
        out_specs=pl.BlockSpec((1, 2, w, LANES * L // 2), lambda i, j: (i, j, 0, 0)),
        out_shape=jax.ShapeDtypeStruct((b, 2 * n_cb, w, LANES * L // 2), BF16),
        scratch_shapes=[pltpu.VMEM((LANES, L, w), F32)],
        compiler_params=_cparams(("parallel", "parallel")),
        name="ssm_utile",
    )(u4, eye)


def _ssm_body(u_ref, tt_ref, pp_ref, qq_ref, dec_ref, y_ref, st_ref, *, n_chunks):
    bsz = u_ref.shape[0]
    n_cb = u_ref.shape[1] // 2
    width = u_ref.shape[2] // 2
    L = SSM_CHUNK
    half = L // 2
    w = LANES
    assert bsz == SUBLANES

    def tile_of(cb, step):
        return 2 * cb + step // half, slice((step % half) * w, (step % half + 1) * w)

    def ucol(b, g):
        pieces = []
        for s in range(L):
            row = []
            for cb in range(n_cb):
                tile, lanes = tile_of(cb, s)
                row.append(u_ref[b, tile, g * width:(g + 1) * width, lanes])
            pieces.append(jnp.concatenate(row, axis=1))
        return jnp.concatenate(pieces, axis=0)

    for b in range(bsz):
        e = _dot_tn(ucol(b, 0), pp_ref[0]) + _dot_tn(ucol(b, 1), pp_ref[1])
        for i in range(4):
            st_ref[i, pl.ds(b, n_chunks, stride=bsz), :] = e[:, i * w:(i + 1) * w]

    dec = dec_ref[0] + dec_ref[1]
    a_fr, a_fi, a_br, a_bi = (jnp.broadcast_to(dec[i:i + 1, :], (bsz, w)) for i in range(4))

    def step(c, carry):
        hfr, hfi, hbr, hbi = carry
        rf = pl.multiple_of(c * bsz, bsz)
        rb = pl.multiple_of((n_chunks - 1 - c) * bsz, bsz)
        efr = st_ref[0, pl.ds(rf, bsz), :]
        efi = st_ref[1, pl.ds(rf, bsz), :]
        ebr = st_ref[2, pl.ds(rb, bsz), :]
        ebi = st_ref[3, pl.ds(rb, bsz), :]
        st_ref[0, pl.ds(rf, bsz), :] = hfr
        st_ref[1, pl.ds(rf, bsz), :] = hfi
        st_ref[2, pl.ds(rb, bsz), :] = hbr
        st_ref[3, pl.ds(rb, bsz), :] = hbi
        return (a_fr * hfr - a_fi * hfi + efr, a_fi * hfr + a_fr * hfi + efi,
                a_br * hbr - a_bi * hbi + ebr, a_bi * hbr + a_br * hbi + ebi)

    zero = jnp.zeros((bsz, w), F32)
    lax.fori_loop(0, n_chunks, step, (zero, zero, zero, zero))

    for b in range(bsz):
        h = jnp.concatenate([st_ref[i, pl.ds(b, n_chunks, stride=bsz), :] for i in range(4)],
                            axis=1).astype(BF16)
        for g in range(2):
            y = (_dot(tt_ref[g], ucol(b, g)) + _dot_nt(qq_ref[g], h)).astype(BF16)
            for t in range(L):
                for cb in range(n_cb):
                    tile, lanes = tile_of(cb, t)
                    y_ref[b, tile, g * width:(g + 1) * width, lanes] = y[t * width:(t + 1) * width, cb * w:(cb + 1) * w]


def _ssm(u_t, tt, pp, qq, dec):
    b, n_tiles, ssm_w, tl = u_t.shape
    g, k, _ = tt.shape
    n_state8 = pp.shape[2]
    width = k // SSM_CHUNK
    n_chunks = n_tiles // 2 * LANES
    pair = lambda shape: pl.BlockSpec((2,) + shape, lambda i: (i,) + (0,) * len(shape))
    seq = pl.BlockSpec((b, n_tiles, 2 * width, tl), lambda i: (0, 0, i, 0))
    return pl.pallas_call(
        functools.partial(_ssm_body, n_chunks=n_chunks),
        grid=(g // 2,),
        in_specs=[seq, pair((k, k)), pair((k, n_state8)), pair((k, n_state8)), pair((SUBLANES, dec.shape[2]))],
        out_specs=seq,
        out_shape=jax.ShapeDtypeStruct(u_t.shape, BF16),
        scratch_shapes=[pltpu.VMEM((n_state8 // LANES, n_chunks * b, LANES), F32)],
        compiler_params=_cparams(("parallel",)),
        name="ssm",
    )(u_t, tt, pp, qq, dec)


def _ssm_post_body(y_ref, wglu_ref, bglu_ref, wos_ref, o_ref):
    ssm_w = y_ref.shape[2]
    y = y_ref[0, 0].astype(F32)
    glu = _dot_tn(jax.nn.gelu(y).astype(BF16), wglu_ref[...]) + bglu_ref[...]
    ssm_out = glu[:, :ssm_w] * jax.nn.sigmoid(glu[:, ssm_w:])
    out = _dot(ssm_out.astype(BF16), wos_ref[...])
    for s in range(SSM_CHUNK // 2):
        o_ref[0, :, s, :] = out[s * LANES:(s + 1) * LANES]


def _ssm_post(y_t, w_glu, b_glu, w_o_ssm):
    b, n_tiles, ssm_w, tl = y_t.shape
    d = w_o_ssm.shape[1]
    L = SSM_CHUNK
    s = n_tiles * tl
    const = lambda shape: pl.BlockSpec(shape, lambda i, j: (0,) * len(shape))
    out = pl.pallas_call(
        _ssm_post_body,
        grid=(b, n_tiles),
        in_specs=[pl.BlockSpec((1, 1, ssm_w, tl), lambda i, j: (i, j, 0, 0)),
                  const(w_glu.shape), const((1, w_glu.shape[1])), const(w_o_ssm.shape)],
        out_specs=pl.BlockSpec((1, LANES, L // 2, d), lambda i, j: (i, j // 2, j % 2, 0)),
        out_shape=jax.ShapeDtypeStruct((b, s // L, L, d), F32),
        compiler_params=_cparams(("parallel", "parallel")),
        name="ssm_post",
    )(y_t, w_glu.astype(BF16), b_glu.reshape(1, -1).astype(F32), w_o_ssm.astype(BF16))
    return out.reshape(b, s, d)


def _merge_ffn_body(h_ref, at_ref, ssm_ref, ga_ref, gs_ref, woa_ref, wout_ref, g_ref, wg_ref, wu_ref, wd_ref,
                    *rest):
    o_ref = rest[-1]
    attn = _dot_tn(at_ref[0], woa_ref[...])
    merged = ga_ref[0].astype(F32) * attn + gs_ref[0].astype(F32) * ssm_ref[0]
    h = h_ref[0] + _dot(merged.astype(BF16), wout_ref[...])
    h = _ffn_apply(h, g_ref, wg_ref, wu_ref, wd_ref)
    if len(rest) == 2:
        h = _rms(h, rest[0][...])
    o_ref[0] = h


def _merge_ffn(h1, attn_t, ssm, ga, gs, w_o_attn, w_out, norm, w_gate, w_up, w_down, final_norm=None):
    b, s, d = h1.shape
    hv = attn_t.shape[1]
    tm = min(TOKEN_TILE, s)
    tok = lambda w: pl.BlockSpec((1, tm, w), lambda i, j: (i, j, 0))
    ffn_args, ffn_specs = _ffn_operands(norm, w_gate, w_up, w_down)
    args = [h1, attn_t, ssm, ga, gs, w_o_attn.astype(BF16), w_out.astype(BF16)] + ffn_args
    in_specs = [tok(d), pl.BlockSpec((1, hv, tm), lambda i, j: (i, 0, j)), tok(d), tok(d), tok(d),
                _resident(w_o_attn.shape), _resident(w_out.shape)] + ffn_specs
    if final_norm is not None:
        args.append(final_norm.reshape(1, d).astype(F32))
        in_specs.append(_resident((1, d)))
    return pl.pallas_call(
        _merge_ffn_body,
        grid=(b, s // tm),
        in_specs=in_specs,
        out_specs=tok(d),
        out_shape=jax.ShapeDtypeStruct((b, s, d), F32),
        compiler_params=_cparams(("parallel", "parallel")),
        name="merge_ffn",
    )(*args)


def kernel(x, positions, ffn1_norm, ffn1_w_gate, ffn1_w_up, ffn1_w_down, mix_norm, w_in, q_norm, w_uq, kv_norm, w_ukv, w_o_attn, ssm_lambda_re, ssm_lambda_im, ssm_log_dt, ssm_b_re, ssm_b_im, ssm_c_re, ssm_c_im, ssm_d, w_glu, b_glu, w_o_ssm, w_out, ffn2_norm, ffn2_w_gate, ffn2_w_up, ffn2_w_down, final_norm):
    b, s, d = x.shape
    depth = ffn1_norm.shape[0]
    _, _, n_grp, _, grp_w = ssm_b_re.shape
    ssm_w = n_grp * grp_w
    L = SSM_CHUNK
    assert depth >= 1, "the final norm is fused into the last layer's second FFN"
    assert b == SUBLANES, "the S5 chunk recurrence keeps the batch on the 8 sublanes"
    assert s % max(TOKEN_TILE, ATTN_TQ, L * LANES) == 0 and L * grp_w == MXU_DIM and n_grp % 2 == 0

    cos_c, sin_c = _rope_tables(positions)
    h = x
    for l in range(depth):
        last = l == depth - 1
        h = _ffn(h.reshape(b * s, d), ffn1_norm[l], ffn1_w_gate[l], ffn1_w_up[l], ffn1_w_down[l]).reshape(b, s, d)
        q, k, vt, u, ga, gs = _inproj(h, cos_c, sin_c, mix_norm[l], w_in[l], q_norm[l], w_uq[l],
                                      kv_norm[l], w_ukv[l], ssm_w)
        attn_t = _attention(q, k, vt)

        tt, pp, qq, dec = _ssm_prep(ssm_lambda_re[l], ssm_lambda_im[l], ssm_log_dt[l], ssm_b_re[l], ssm_b_im[l],
                                    ssm_c_re[l], ssm_c_im[l], ssm_d[l])
        y_t = _ssm(_utile(u), tt, pp, qq, dec)
        ssm = _ssm_post(y_t, w_glu[l], b_glu[l], w_o_ssm[l])

        h = _merge_ffn(h, attn_t, ssm, ga, gs, w_o_attn[l], w_out[l], ffn2_norm[l], ffn2_w_gate[l],
                       ffn2_w_up[l], ffn2_w_down[l], final_norm if last else None)
    return h
```

```python
import functools
import math

import jax
import jax.numpy as jnp
from jax import lax
from jax.experimental import pallas as pl
from jax.experimental.pallas import tpu as pltpu

F32 = jnp.float32
BF16 = jnp.bfloat16

MLA_HEADS = 8
QK_NOPE_DIM = 64
QK_ROPE_DIM = 32
V_HEAD_DIM = 64
ROPE_THETA = 10000.0
FFN_RES = 0.5
EPS = 1e-6

LANES = 128
SUBLANES = 8
MXU_DIM = 256
VMEM_LIMIT_BYTES = 56 * 1024 * 1024

HEAD_PAD = LANES
V_ONES_ROWS = 16
SSM_CHUNK = 16

TOKEN_TILE = 512
ATTN_TQ = 1024
FF_CHUNK = 256
ATTN_MAX_GAP = 64.0

def _cparams(sem, flags=None):
    return pltpu.CompilerParams(dimension_semantics=sem, vmem_limit_bytes=VMEM_LIMIT_BYTES, flags=flags)


def _rms(x, g):
    return x * lax.rsqrt(jnp.mean(x * x, axis=-1, keepdims=True) + EPS) * g


def _dot(a, b):
    return jnp.dot(a, b, preferred_element_type=F32)


def _dot_nt(a, b):
    return lax.dot_general(a, b, (((1,), (1,)), ((), ())), preferred_element_type=F32)


def _dot_tn(a, b):
    return lax.dot_general(a, b, (((0,), (0,)), ((), ())), preferred_element_type=F32)


def _dot_exact(a, b):
    return jnp.dot(a, b, preferred_element_type=F32, precision=lax.Precision.HIGHEST)


def _bf16_parts(a):
    hi = a.astype(BF16)
    rest = a - hi.astype(F32)
    mid = rest.astype(BF16)
    lo = (rest - mid.astype(F32)).astype(BF16)
    return jnp.concatenate([hi, mid, lo], axis=0)


def _rope_body(pos_ref, invf_ref, o_ref):
    half = invf_ref.shape[0]
    ang = invf_ref[...] * pos_ref[0]
    o_ref[0, :half, :] = jnp.cos(ang)
    o_ref[0, half:, :] = jnp.sin(ang)


def _rope_tables(positions):
    b, s = positions.shape
    half = QK_ROPE_DIM // 2
    inv_freq = ROPE_THETA ** (-jnp.arange(0, QK_ROPE_DIM, 2, dtype=F32) / QK_ROPE_DIM)
    return pl.pallas_call(
        _rope_body,
        grid=(b,),
        in_specs=[pl.BlockSpec((1, 1, s), lambda i: (i, 0, 0)), pl.BlockSpec((half, 1), lambda i: (0, 0))],
        out_specs=pl.BlockSpec((1, 2 * half, s), lambda i: (i, 0, 0)),
        out_shape=jax.ShapeDtypeStruct((b, 2 * half, s), F32),
        compiler_params=_cparams(("parallel",)),
        name="rope_tables",
    )(positions.astype(F32).reshape(b, 1, s), inv_freq.reshape(half, 1))


def _ffn_apply(x, g_ref, wg_ref, wu_ref, wd_ref):
    n = _rms(x, g_ref[...]).astype(BF16)
    acc = jnp.zeros_like(x)
    for j in range(wg_ref.shape[1] // FF_CHUNK):
        cols = slice(j * FF_CHUNK, (j + 1) * FF_CHUNK)
        gate = _dot(n, wg_ref[:, cols])
        up = _dot(n, wu_ref[:, cols])
        act = (gate * jax.nn.sigmoid(gate) * up).astype(BF16)
        acc = acc + _dot(act, wd_ref[cols, :])
    return x + FFN_RES * acc


def _ffn_body(x_ref, g_ref, wg_ref, wu_ref, wd_ref, o_ref):
    o_ref[...] = _ffn_apply(x_ref[...], g_ref, wg_ref, wu_ref, wd_ref)


def _resident(shape):
    return pl.BlockSpec(shape, lambda *_: (0,) * len(shape), pipeline_mode=pl.Buffered(1))


def _ffn_operands(norm, w_gate, w_up, w_down):
    d, d_ff = w_gate.shape
    assert d_ff % FF_CHUNK == 0
    args = [norm.reshape(1, d).astype(F32), w_gate.astype(BF16), w_up.astype(BF16), w_down.astype(BF16)]
    return args, [_resident(a.shape) for a in args]


def _ffn(x2, norm, w_gate, w_up, w_down):
    t, d = x2.shape
    tm = min(TOKEN_TILE, t)
    row = pl.BlockSpec((tm, d), lambda i: (i, 0))
    args, specs = _ffn_operands(norm, w_gate, w_up, w_down)
    return pl.pallas_call(
        _ffn_body,
        grid=(t // tm,),
        in_specs=[row] + specs,
        out_specs=row,
        out_shape=jax.ShapeDtypeStruct((t, d), F32),
        compiler_params=_cparams(("parallel",)),
        name="ffn",
    )(x2, *args)


def _inproj_body(h_ref, g_ref, win_ref, qn_ref, kvn_ref, wq_ref, wk_ref, wvt_ref,
                 rope_ref, place_ref, base_ref,
                 q_ref, k_ref, vt_ref, u_ref, ga_ref, gs_ref, *, dims):
    q_rank, kv_rank, ssm_w, d_model, scale = dims
    n = _rms(h_ref[0], g_ref[...]).astype(BF16)
    z = _dot(n, win_ref[...])
    o = 0
    c_q = z[:, o:o + q_rank]; o += q_rank
    c_kv = z[:, o:o + kv_rank]; o += kv_rank
    kr = z[:, o:o + HEAD_PAD]; o += HEAD_PAD
    u = z[:, o:o + ssm_w]; o += ssm_w
    g_attn = z[:, o:o + d_model]; o += d_model
    g_ssm = z[:, o:o + d_model]

    u_ref[0] = u.astype(BF16)
    ga_ref[0] = jax.nn.sigmoid(g_attn).astype(BF16)
    gs_ref[0] = jax.nn.sigmoid(g_ssm).astype(BF16)

    cos_sin = _dot_tn(_bf16_parts(rope_ref[0]), place_ref[...])
    cos = cos_sin[:, :HEAD_PAD] + base_ref[...]
    sin = cos_sin[:, HEAD_PAD:]

    cqn = _rms(c_q, qn_ref[...]).astype(BF16)
    q_all = _dot(cqn, wq_ref[...])
    q_rot = pltpu.roll(q_all, q_all.shape[1] - QK_ROPE_DIM, axis=1)
    ckvn = _rms(c_kv, kvn_ref[...]).astype(BF16)
    k_nope = _dot(ckvn, wk_ref[...])
    k_rope = kr * cos + pltpu.roll(kr, HEAD_PAD - QK_ROPE_DIM, axis=1) * sin
    for h in range(MLA_HEADS):
        sl = slice(h * HEAD_PAD, (h + 1) * HEAD_PAD)
        q_ref[0, h] = ((q_all[:, sl] * cos + q_rot[:, sl] * sin) * scale).astype(BF16)
        k_ref[0, h] = (k_nope[:, sl] + k_rope).astype(BF16)
    vt = _dot_nt(wvt_ref[...], ckvn).astype(BF16)
    ones = jnp.ones((V_ONES_ROWS, vt.shape[1]), BF16)
    vt_ref[0, 0] = jnp.concatenate(
        [piece for h in range(MLA_HEADS) for piece in (vt[h * V_HEAD_DIM:(h + 1) * V_HEAD_DIM], ones)], axis=0)


def _place_cols(w, starts, width, total):
    out = jnp.zeros((w.shape[0], total), w.dtype)
    for i, s in enumerate(starts):
        out = lax.dynamic_update_slice(out, w[:, i * width:(i + 1) * width], (0, s))
    return out


def _rot_half(w):
    half = w.shape[1] // 2
    return jnp.concatenate([-w[:, half:], w[:, :half]], axis=1)


def _inproj(h1, rope_t, mix_norm, w_in, q_norm, w_uq, kv_norm, w_ukv, ssm_w):
    b, s, d = h1.shape
    q_rank, kv_rank = q_norm.shape[0], kv_norm.shape[0]
    hq = QK_NOPE_DIM + QK_ROPE_DIM
    hkv = QK_NOPE_DIM + V_HEAD_DIM
    scale = float(hq) ** -0.5 * math.log2(math.e)
    o_kr = q_rank + kv_rank
    o_u = o_kr + QK_ROPE_DIM
    w_kr = w_in[:, o_kr:o_u]
    assert hq + QK_ROPE_DIM == HEAD_PAD
    kr_p = jnp.concatenate([jnp.zeros((d, QK_NOPE_DIM), w_in.dtype), w_kr, _rot_half(w_kr)], axis=1)
    win = jnp.concatenate([w_in[:, :o_kr], kr_p, w_in[:, o_u:]], axis=1).astype(BF16)
    wq = jnp.concatenate(
        [piece for h in range(MLA_HEADS)
         for piece in (w_uq[:, h * hq:(h + 1) * hq], _rot_half(w_uq[:, h * hq + QK_NOPE_DIM:(h + 1) * hq]))],
        axis=1).astype(BF16)
    wk = jnp.concatenate([jnp.pad(w_ukv[:, h * hkv:h * hkv + QK_NOPE_DIM], ((0, 0), (0, HEAD_PAD - QK_NOPE_DIM)))
                          for h in range(MLA_HEADS)], axis=1).astype(BF16)
    wvt = jnp.concatenate([w_ukv[:, h * hkv + QK_NOPE_DIM:(h + 1) * hkv] for h in range(MLA_HEADS)],
                          axis=1).T.astype(BF16)
    half = QK_ROPE_DIM // 2
    eye = jnp.eye(half, dtype=F32)
    place1 = _place_cols(jnp.concatenate([eye, eye], axis=1), [QK_NOPE_DIM], QK_ROPE_DIM, HEAD_PAD)
    zero = jnp.zeros_like(place1)
    place = jnp.concatenate([jnp.concatenate([place1, zero], axis=1),
                             jnp.concatenate([zero, place1], axis=1)], axis=0)
    place = jnp.tile(place, (3, 1)).astype(BF16)
    base = (jnp.arange(HEAD_PAD) < QK_NOPE_DIM).astype(F32).reshape(1, HEAD_PAD)

    tm = min(TOKEN_TILE, s)
    nt = s // tm
    hv = MLA_HEADS * (V_HEAD_DIM + V_ONES_ROWS)
    const = lambda shape: pl.BlockSpec(shape, lambda i, j: (0,) * len(shape))
    tok = lambda w: pl.BlockSpec((1, tm, w), lambda i, j: (i, j, 0))
    heads = pl.BlockSpec((1, MLA_HEADS, tm, HEAD_PAD), lambda i, j: (i, 0, j, 0))
    rope = pl.BlockSpec((1, 2 * half, tm), lambda i, j: (i, 0, j))
    dims = (q_rank, kv_rank, ssm_w, d, scale)
    return pl.pallas_call(
        functools.partial(_inproj_body, dims=dims),
        grid=(b, nt),
        in_specs=[tok(d), const((1, d)), const(win.shape), const((1, q_rank)), const((1, kv_rank)),
                  const(wq.shape), const(wk.shape), const(wvt.shape),
                  rope, const(place.shape), const(base.shape)],
        out_specs=[heads, heads, pl.BlockSpec((1, 1, hv, tm), lambda i, j: (i, j, 0, 0)),
                   tok(ssm_w), tok(d), tok(d)],
        out_shape=[jax.ShapeDtypeStruct((b, MLA_HEADS, s, HEAD_PAD), BF16),
                   jax.ShapeDtypeStruct((b, MLA_HEADS, s, HEAD_PAD), BF16),
                   jax.ShapeDtypeStruct((b, nt, hv, tm), BF16),
                   jax.ShapeDtypeStruct((b, s, ssm_w), BF16),
                   jax.ShapeDtypeStruct((b, s, d), BF16),
                   jax.ShapeDtypeStruct((b, s, d), BF16)],
        compiler_params=_cparams(("parallel", "parallel")),
        name="inproj",
    )(h1, mix_norm.reshape(1, d).astype(F32), win, q_norm.reshape(1, q_rank).astype(F32),
      kv_norm.reshape(1, kv_rank).astype(F32), wq, wk, wvt, rope_t, place, base)


def _attn_body(q_ref, k_ref, vt_ref, o_ref, s_ref, acc_ref, *, n_kt, tk):
    q = q_ref[0, 0]
    tq = q.shape[0]

    def finish():
        acc = acc_ref[...]
        o_ref[0] = (acc[:V_HEAD_DIM] / acc[V_HEAD_DIM:V_HEAD_DIM + 1]).astype(BF16)

    s = _dot_nt(k_ref[0, 0, 0:tk, :], q)
    m = jnp.max(s, axis=0, keepdims=True)
    acc_ref[...] = _dot(vt_ref[0, 0], jnp.exp2(s - m).astype(BF16))
    gap = jnp.zeros_like(m)
    for j in range(1, n_kt):
        s = _dot_nt(k_ref[0, 0, j * tk:(j + 1) * tk, :], q)
        p = jnp.exp2(s - m).astype(BF16)
        m_tile = jnp.max(s, axis=0, keepdims=True)
        gap = jnp.maximum(gap, m_tile - m)
        m_new = jnp.maximum(m, m_tile)
        acc_ref[...] = (acc_ref[...] + _dot(vt_ref[0, j], p)) * jnp.exp2(m - m_new)
        m = m_new
    finish()

    @pl.when(jnp.max(gap) > ATTN_MAX_GAP)
    def _():
        def tile(j, m):
            start = pl.multiple_of(j * tk, tk)
            s_ref[...] = _dot_nt(k_ref[0, 0, pl.ds(start, tk), :], q)
            s = s_ref[...]
            m_new = jnp.maximum(m, jnp.max(s, axis=0, keepdims=True))
            p = jnp.exp2(s - m_new).astype(BF16)
            acc_ref[...] = jnp.exp2(m - m_new) * acc_ref[...] + _dot(vt_ref[0, j], p)
            return m_new

        acc_ref[...] = jnp.zeros_like(acc_ref)
        lax.fori_loop(0, n_kt, tile, jnp.full((1, tq), -jnp.inf, F32))
        finish()


def _attention(q, k, vt):
    b, h, s, _ = q.shape
    n_kt, tk = vt.shape[1], vt.shape[3]
    vrows = vt.shape[2] // h
    tq = min(ATTN_TQ, s)
    return pl.pallas_call(
        functools.partial(_attn_body, n_kt=n_kt, tk=tk),
        grid=(b, h, s // tq),
        in_specs=[pl.BlockSpec((1, 1, tq, HEAD_PAD), lambda i, j, t: (i, j, t, 0)),
                  pl.BlockSpec((1, 1, s, HEAD_PAD), lambda i, j, t: (i, j, 0, 0)),
                  pl.BlockSpec((1, n_kt, vrows, tk), lambda i, j, t: (i, 0, j, 0))],
        out_specs=pl.BlockSpec((1, V_HEAD_DIM, tq), lambda i, j, t: (i, j, t)),
        out_shape=jax.ShapeDtypeStruct((b, h * V_HEAD_DIM, s), BF16),
        scratch_shapes=[pltpu.VMEM((tk, tq), F32), pltpu.VMEM((vrows, tq), F32)],
        compiler_params=_cparams(("parallel", "parallel", "arbitrary")),
        name="attention",
    )(q, k, vt)


def _discretise(lr, li, ldt):
    dt = jnp.exp(ldt)
    mag = jnp.exp(lr * dt)
    lb_re = mag * jnp.cos(li * dt)
    lb_im = mag * jnp.sin(li * dt)
    den = lr * lr + li * li
    nr = lb_re - 1.0
    return (nr * lr + lb_im * li) / den, (lb_im * lr - nr * li) / den, lr * dt, li * dt


def _ssm_prep_body(rowp_ref, colp_ref, bt_re_ref, bt_im_ref, bc_re_ref, bc_im_ref, cr_re_ref, cr_im_ref, dt_ref,
                   tt_ref, pp_ref, qq_ref, dec_ref, *, n_state):
    L = SSM_CHUNK
    rows = tt_ref.shape[1]
    width = rows // L
    lanes2 = 2 * n_state
    parity = pl.program_id(0) % 2
    lane = lax.broadcasted_iota(jnp.int32, (1, lanes2), 1)
    keep_lane = (lane >= parity * n_state) & (lane < (parity + 1) * n_state)
    first_lane = lane < n_state
    n_pow = (L + 1 + SUBLANES - 1) // SUBLANES * SUBLANES
    kpow = lax.broadcasted_iota(jnp.int32, (n_pow, 1), 0).astype(F32)
    rep = (lax.broadcasted_iota(jnp.int32, (rows + width, n_pow), 0) // width
           == lax.broadcasted_iota(jnp.int32, (rows + width, n_pow), 1)).astype(F32)
    sl = lax.broadcasted_iota(jnp.int32, (1, rows), 1) // width
    blk = lambda a, k: a[k * width:(k + 1) * width]

    k_rows = []
    for d in range(2):
        k_re, k_im, ar, ai = _discretise(rowp_ref[0, d, 0:1, :], rowp_ref[0, d, 1:2, :], rowp_ref[0, d, 2:3, :])
        magk = jnp.exp(kpow * ar)
        ak_re = _dot_exact(rep, magk * jnp.cos(kpow * ai))
        ak_im = _dot_exact(rep, magk * jnp.sin(kpow * ai))
        bt_re, bt_im = bt_re_ref[0, d], bt_im_ref[0, d]
        bb_re = k_re * bt_re - k_im * bt_im
        bb_im = k_re * bt_im + k_im * bt_re
        ba_re = bb_re * ak_re[:rows] - bb_im * ak_im[:rows]
        ba_im = bb_re * ak_im[:rows] + bb_im * ak_re[:rows]
        blocks = range(L - 1, -1, -1) if d == 0 else range(L)
        p_re = jnp.concatenate([blk(ba_re, k) for k in blocks], axis=0)
        p_im = jnp.concatenate([blk(ba_im, k) for k in blocks], axis=0)
        pp_ref[0, :, (2 * d) * lanes2:(2 * d + 1) * lanes2] = jnp.where(keep_lane, p_re, 0.0).astype(BF16)
        pp_ref[0, :, (2 * d + 1) * lanes2:(2 * d + 2) * lanes2] = jnp.where(keep_lane, p_im, 0.0).astype(BF16)
        cr_re, cr_im = cr_re_ref[0, d], cr_im_ref[0, d]
        ca_re = cr_re * ak_re - cr_im * ak_im
        ca_im = cr_re * ak_im + cr_im * ak_re
        blocks = range(1, L + 1) if d == 0 else range(L, 0, -1)
        q_re = jnp.concatenate([blk(ca_re, k) for k in blocks], axis=0)
        q_im = jnp.concatenate([blk(ca_im, k) for k in blocks], axis=0)
        qq_ref[0, :, (2 * d) * lanes2:(2 * d + 1) * lanes2] = jnp.where(keep_lane, q_re, 0.0).astype(BF16)
        qq_ref[0, :, (2 * d + 1) * lanes2:(2 * d + 2) * lanes2] = jnp.where(keep_lane, -q_im, 0.0).astype(BF16)
        kc_re, kc_im, _, _ = _discretise(colp_ref[0, d, :, 0:1], colp_ref[0, d, :, 1:2], colp_ref[0, d, :, 2:3])
        bc_re, bc_im = bc_re_ref[0, d], bc_im_ref[0, d]
        bbc_re = kc_re * bc_re - kc_im * bc_im
        bbc_im = kc_re * bc_im + kc_im * bc_re
        k_rows.append(_dot_exact(jnp.where(first_lane, ca_re[:rows], 0.0), bbc_re)
                      - _dot_exact(jnp.where(first_lane, ca_im[:rows], 0.0), bbc_im))
        dec_ref[0, 2 * d:2 * d + 1, :] = jnp.where(keep_lane, ak_re[rows:rows + 1], 0.0)
        dec_ref[0, 2 * d + 1:2 * d + 2, :] = jnp.where(keep_lane, ak_im[rows:rows + 1], 0.0)
    dec_ref[0, 4:8, :] = jnp.zeros((4, lanes2), F32)

    kf, kb = k_rows
    diag = blk(kf, 0) + blk(kb, 0) + dt_ref[0]
    for t in range(L):
        acc = jnp.where(sl == t, diag, 0.0)
        for k in range(1, t + 1):
            acc = jnp.where(sl == t - k, blk(kf, k), acc)
        for k in range(1, L - t):
            acc = jnp.where(sl == t + k, blk(kb, k), acc)
        tt_ref[0, t * width:(t + 1) * width, :] = acc.astype(BF16)


def _ssm_prep(lam_re, lam_im, log_dt, b_re, b_im, c_re, c_im, d_skip):
    _, g, p, w = b_re.shape
    L = SSM_CHUNK
    rows = L * w
    dup = lambda a, axis: jnp.concatenate([a, a], axis=axis)
    ldt = jnp.broadcast_to(log_dt[:, :, None], lam_re.shape)
    params = [dup(lam_re, -1), dup(lam_im, -1), dup(ldt, -1)]
    rowp = jnp.pad(jnp.stack(params, axis=2), ((0, 0), (0, 0), (0, SUBLANES - 3), (0, 0)))
    colp = jnp.pad(jnp.stack(params, axis=3), ((0, 0), (0, 0), (0, 0), (0, LANES - 3)))
    per_group = lambda a: jnp.swapaxes(a, 0, 1).astype(F32)
    bt = lambda b: jnp.tile(dup(jnp.swapaxes(b, 2, 3), -1), (1, 1, L, 1))
    bc = lambda b: jnp.tile(dup(b, 2), (1, 1, 1, L))
    cr = lambda c: jnp.tile(dup(c, -1), (1, 1, L + 1, 1))
    dmat = jnp.where(jnp.eye(w, dtype=bool)[None], d_skip.reshape(g, 1, w), 0.0)
    dtile = jnp.tile(dmat, (1, 1, L)).astype(F32)
    grp = lambda shape: pl.BlockSpec((1,) + shape, lambda i: (i,) + (0,) * len(shape))
    return pl.pallas_call(
        functools.partial(_ssm_prep_body, n_state=p),
        grid=(g,),
        in_specs=[grp((2, SUBLANES, 2 * p)), grp((2, 2 * p, LANES)), grp((2, rows, 2 * p)), grp((2, rows, 2 * p)),
                  grp((2, 2 * p, rows)), grp((2, 2 * p, rows)), grp((2, rows + w, 2 * p)), grp((2, rows + w, 2 * p)),
                  grp((w, rows))],
        out_specs=[grp((rows, rows)), grp((rows, 8 * p)), grp((rows, 8 * p)), grp((SUBLANES, 2 * p))],
        out_shape=[jax.ShapeDtypeStruct((g, rows, rows), BF16),
                   jax.ShapeDtypeStruct((g, rows, 8 * p), BF16),
                   jax.ShapeDtypeStruct((g, rows, 8 * p), BF16),
                   jax.ShapeDtypeStruct((g, SUBLANES, 2 * p), F32)],
        compiler_params=_cparams(("parallel",)),
        name="ssm_prep",
    )(per_group(rowp), per_group(colp), per_group(bt(b_re)), per_group(bt(b_im)), per_group(bc(b_re)),
      per_group(bc(b_im)), per_group(cr(c_re)), per_group(cr(c_im)), dtile)


def _utile_body(u_ref, eye_ref, o_ref, scr_ref):
    scr_ref[...] = u_ref[0].astype(F32)
    half = SSM_CHUNK // 2
    for sh in range(2):
        rows = jnp.concatenate([scr_ref[:, sh * half + s, :] for s in range(half)], axis=0).astype(BF16)
        o_ref[0, sh] = _dot_nt(eye_ref[...], rows).astype(BF16)


def _utile(u):
    b, s, w = u.shape
    L = SSM_CHUNK
    n_cb = s // (L * LANES)
    u4 = u.reshape(b, s // L, L, w)
    eye = jnp.eye(w, dtype=BF16)
    return pl.pallas_call(
        _utile_body,
        grid=(b, n_cb),
        in_specs=[pl.BlockSpec((1, LANES, L, w), lambda i, j: (i, j, 0, 0)),
                  pl.BlockSpec((w, w), lambda i, j: (0, 0))],
        out_specs=pl.BlockSpec((1, 2, w, LANES * L // 2), lambda i, j: (i, j, 0, 0)),
        out_shape=jax.ShapeDtypeStruct((b, 2 * n_cb, w, LANES * L // 2), BF16),
        scratch_shapes=[pltpu.VMEM((LANES, L, w), F32)],
        compiler_params=_cparams(("parallel", "parallel")),
        name="ssm_utile",
    )(u4, eye)


def _ssm_body(u_ref, tt_ref, pp_ref, qq_ref, dec_ref, y_ref, st_ref, *, n_chunks):
    bsz = u_ref.shape[0]
    n_cb = u_ref.shape[1] // 2
    width = u_ref.shape[2] // 2
    L = SSM_CHUNK
    half = L // 2
    w = LANES
    assert bsz == SUBLANES

    def tile_of(cb, step):
        return 2 * cb + step // half, slice((step % half) * w, (step % half + 1) * w)

    def ucol(b, g):
        pieces = []
        for s in range(L):
            row = []
            for cb in range(n_cb):
                tile, lanes = tile_of(cb, s)
                row.append(u_ref[b, tile, g * width:(g + 1) * width, lanes])
            pieces.append(jnp.concatenate(row, axis=1))
        return jnp.concatenate(pieces, axis=0)

    for b in range(bsz):
        e = _dot_tn(ucol(b, 0), pp_ref[0]) + _dot_tn(ucol(b, 1), pp_ref[1])
        for i in range(4):
            st_ref[i, pl.ds(b, n_chunks, stride=bsz), :] = e[:, i * w:(i + 1) * w]

    dec = dec_ref[0] + dec_ref[1]
    a_fr, a_fi, a_br, a_bi = (jnp.broadcast_to(dec[i:i + 1, :], (bsz, w)) for i in range(4))

    def step(c, carry):
        hfr, hfi, hbr, hbi = carry
        rf = pl.multiple_of(c * bsz, bsz)
        rb = pl.multiple_of((n_chunks - 1 - c) * bsz, bsz)
        efr = st_ref[0, pl.ds(rf, bsz), :]
        efi = st_ref[1, pl.ds(rf, bsz), :]
        ebr = st_ref[2, pl.ds(rb, bsz), :]
        ebi = st_ref[3, pl.ds(rb, bsz), :]
        st_ref[0, pl.ds(rf, bsz), :] = hfr
        st_ref[1, pl.ds(rf, bsz), :] = hfi
        st_ref[2, pl.ds(rb, bsz), :] = hbr
        st_ref[3, pl.ds(rb, bsz), :] = hbi
        return (a_fr * hfr - a_fi * hfi + efr, a_fi * hfr + a_fr * hfi + efi,
                a_br * hbr - a_bi * hbi + ebr, a_bi * hbr + a_br * hbi + ebi)

    zero = jnp.zeros((bsz, w), F32)
    lax.fori_loop(0, n_chunks, step, (zero, zero, zero, zero))

    for b in range(bsz):
        h = jnp.concatenate([st_ref[i, pl.ds(b, n_chunks, stride=bsz), :] for i in range(4)],
                            axis=1).astype(BF16)
        for g in range(2):
            y = (_dot(tt_ref[g], ucol(b, g)) + _dot_nt(qq_ref[g], h)).astype(BF16)
            for t in range(L):
                for cb in range(n_cb):
                    tile, lanes = tile_of(cb, t)
                    y_ref[b, tile, g * width:(g + 1) * width, lanes] = y[t * width:(t + 1) * width, cb * w:(cb + 1) * w]


def _ssm(u_t, tt, pp, qq, dec):
    b, n_tiles, ssm_w, tl = u_t.shape
    g, k, _ = tt.shape
    n_state8 = pp.shape[2]
    width = k // SSM_CHUNK
    n_chunks = n_tiles // 2 * LANES
    pair = lambda shape: pl.BlockSpec((2,) + shape, lambda i: (i,) + (0,) * len(shape))
    seq = pl.BlockSpec((b, n_tiles, 2 * width, tl), lambda i: (0, 0, i, 0))
    return pl.pallas_call(
        functools.partial(_ssm_body, n_chunks=n_chunks),
        grid=(g // 2,),
        in_specs=[seq, pair((k, k)), pair((k, n_state8)), pair((k, n_state8)), pair((SUBLANES, dec.shape[2]))],
        out_specs=seq,
        out_shape=jax.ShapeDtypeStruct(u_t.shape, BF16),
        scratch_shapes=[pltpu.VMEM((n_state8 // LANES, n_chunks * b, LANES), F32)],
        compiler_params=_cparams(("parallel",)),
        name="ssm",
    )(u_t, tt, pp, qq, dec)


def _ssm_post_body(y_ref, wglu_ref, bglu_ref, wos_ref, o_ref):
    ssm_w = y_ref.shape[2]
    y = y_ref[0, 0].astype(F32)
    glu = _dot_tn(jax.nn.gelu(y).astype(BF16), wglu_ref[...]) + bglu_ref[...]
    ssm_out = glu[:, :ssm_w] * jax.nn.sigmoid(glu[:, ssm_w:])
    out = _dot(ssm_out.astype(BF16), wos_ref[...])
    for s in range(SSM_CHUNK // 2):
        o_ref[0, :, s, :] = out[s * LANES:(s + 1) * LANES]


def _ssm_post(y_t, w_glu, b_glu, w_o_ssm):
    b, n_tiles, ssm_w, tl = y_t.shape
    d = w_o_ssm.shape[1]
    L = SSM_CHUNK
    s = n_tiles * tl
    const = lambda shape: pl.BlockSpec(shape, lambda i, j: (0,) * len(shape))
    out = pl.pallas_call(
        _ssm_post_body,
        grid=(b, n_tiles),
        in_specs=[pl.BlockSpec((1, 1, ssm_w, tl), lambda i, j: (i, j, 0, 0)),
                  const(w_glu.shape), const((1, w_glu.shape[1])), const(w_o_ssm.shape)],
        out_specs=pl.BlockSpec((1, LANES, L // 2, d), lambda i, j: (i, j // 2, j % 2, 0)),
        out_shape=jax.ShapeDtypeStruct((b, s // L, L, d), F32),
        compiler_params=_cparams(("parallel", "parallel")),
        name="ssm_post",
    )(y_t, w_glu.astype(BF16), b_glu.reshape(1, -1).astype(F32), w_o_ssm.astype(BF16))
    return out.reshape(b, s, d)


def _merge_ffn_body(h_ref, at_ref, ssm_ref, ga_ref, gs_ref, woa_ref, wout_ref, g_ref, wg_ref, wu_ref, wd_ref,
                    *rest):
    o_ref = rest[-1]
    attn = _dot_tn(at_ref[0], woa_ref[...])
    merged = ga_ref[0].astype(F32) * attn + gs_ref[0].astype(F32) * ssm_ref[0]
    h = h_ref[0] + _dot(merged.astype(BF16), wout_ref[...])
    h = _ffn_apply(h, g_ref, wg_ref, wu_ref, wd_ref)
    if len(rest) == 2:
        h = _rms(h, rest[0][...])
    o_ref[0] = h


def _merge_ffn(h1, attn_t, ssm, ga, gs, w_o_attn, w_out, norm, w_gate, w_up, w_down, final_norm=None):
    b, s, d = h1.shape
    hv = attn_t.shape[1]
    tm = min(TOKEN_TILE, s)
    tok = lambda w: pl.BlockSpec((1, tm, w), lambda i, j: (i, j, 0))
    ffn_args, ffn_specs = _ffn_operands(norm, w_gate, w_up, w_down)
    args = [h1, attn_t, ssm, ga, gs, w_o_attn.astype(BF16), w_out.astype(BF16)] + ffn_args
    in_specs = [tok(d), pl.BlockSpec((1, hv, tm), lambda i, j: (i, 0, j)), tok(d), tok(d), tok(d),
                _resident(w_o_attn.shape), _resident(w_out.shape)] + ffn_specs
    if final_norm is not None:
        args.append(final_norm.reshape(1, d).astype(F32))
        in_specs.append(_resident((1, d)))
    return pl.pallas_call(
        _merge_ffn_body,
        grid=(b, s // tm),
        in_specs=in_specs,
        out_specs=tok(d),
        out_shape=jax.ShapeDtypeStruct((b, s, d), F32),
        compiler_params=_cparams(("parallel", "parallel")),
        name="merge_ffn",
    )(*args)


def kernel(x, positions, ffn1_norm, ffn1_w_gate, ffn1_w_up, ffn1_w_down, mix_norm, w_in, q_norm, w_uq, kv_norm, w_ukv, w_o_attn, ssm_lambda_re, ssm_lambda_im, ssm_log_dt, ssm_b_re, ssm_b_im, ssm_c_re, ssm_c_im, ssm_d, w_glu, b_glu, w_o_ssm, w_out, ffn2_norm, ffn2_w_gate, ffn2_w_up, ffn2_w_down, final_norm):
    b, s, d = x.shape
    depth = ffn1_norm.shape[0]
    _, _, n_grp, _, grp_w = ssm_b_re.shape
    ssm_w = n_grp * grp_w
    L = SSM_CHUNK
    assert depth >= 1, "the final norm is fused into the last layer's second FFN"
    assert b == SUBLANES, "the S5 chunk recurrence keeps the batch on the 8 sublanes"
    assert s % max(TOKEN_TILE, ATTN_TQ, L * LANES) == 0 and L * grp_w == MXU_DIM and n_grp % 2 == 0

    rope_t = _rope_tables(positions)
    h = x
    for l in range(depth):
        last = l == depth - 1
        h = _ffn(h.reshape(b * s, d), ffn1_norm[l], ffn1_w_gate[l], ffn1_w_up[l], ffn1_w_down[l]).reshape(b, s, d)
        q, k, vt, u, ga, gs = _inproj(h, rope_t, mix_norm[l], w_in[l], q_norm[l], w_uq[l],
                                      kv_norm[l], w_ukv[l], ssm_w)
        attn_t = _attention(q, k, vt)

        tt, pp, qq, dec = _ssm_prep(ssm_lambda_re[l], ssm_lambda_im[l], ssm_log_dt[l], ssm_b_re[l], ssm_b_im[l],
                                    ssm_c_re[l], ssm_c_im[l], ssm_d[l])
        y_t = _ssm(_utile(u), tt, pp, qq, dec)
        ssm = _ssm_post(y_t, w_glu[l], b_glu[l], w_o_ssm[l])

        h = _merge_ffn(h, attn_t, ssm, ga, gs, w_o_attn[l], w_out[l], ffn2_norm[l], ffn2_w_gate[l],
                       ffn2_w_up[l], ffn2_w_down[l], final_norm if last else None)
    return h
```

```python
import functools
import math

import jax
import jax.numpy as jnp
from jax import lax
from jax.experimental import pallas as pl
from jax.experimental.pallas import tpu as pltpu

F32 = jnp.float32
BF16 = jnp.bfloat16

MLA_HEADS = 8
QK_NOPE_DIM = 64
QK_ROPE_DIM = 32
V_HEAD_DIM = 64
ROPE_THETA = 10000.0
FFN_RES = 0.5
EPS = 1e-6

LANES = 128
SUBLANES = 8
MXU_DIM = 256
VMEM_LIMIT_BYTES = 56 * 1024 * 1024

HEAD_PAD = LANES
AUG_LANE = 96
V_ONES_ROWS = 16
SSM_CHUNK = 16

TOKEN_TILE = 512
ATTN_TQ = 1024
FF_CHUNK = 256
ATTN_MAX_GAP = 64.0

def _cparams(sem, flags=None):
    return pltpu.CompilerParams(dimension_semantics=sem, vmem_limit_bytes=VMEM_LIMIT_BYTES, flags=flags)


def _rms(x, g):
    return x * lax.rsqrt(jnp.mean(x * x, axis=-1, keepdims=True) + EPS) * g


def _dot(a, b):
    return jnp.dot(a, b, preferred_element_type=F32)


def _dot_nt(a, b):
    return lax.dot_general(a, b, (((1,), (1,)), ((), ())), preferred_element_type=F32)


def _dot_tn(a, b):
    return lax.dot_general(a, b, (((0,), (0,)), ((), ())), preferred_element_type=F32)


def _dot_exact(a, b):
    return jnp.dot(a, b, preferred_element_type=F32, precision=lax.Precision.HIGHEST)


def _bf16_parts(a):
    hi = a.astype(BF16)
    rest = a - hi.astype(F32)
    mid = rest.astype(BF16)
    lo = (rest - mid.astype(F32)).astype(BF16)
    return jnp.concatenate([hi, mid, lo], axis=0)


def _rope_body(pos_ref, invf_ref, o_ref):
    half = invf_ref.shape[0]
    ang = invf_ref[...] * pos_ref[0]
    o_ref[0, :half, :] = jnp.cos(ang)
    o_ref[0, half:, :] = jnp.sin(ang)


def _rope_tables(positions):
    b, s = positions.shape
    half = QK_ROPE_DIM // 2
    inv_freq = ROPE_THETA ** (-jnp.arange(0, QK_ROPE_DIM, 2, dtype=F32) / QK_ROPE_DIM)
    return pl.pallas_call(
        _rope_body,
        grid=(b,),
        in_specs=[pl.BlockSpec((1, 1, s), lambda i: (i, 0, 0)), pl.BlockSpec((half, 1), lambda i: (0, 0))],
        out_specs=pl.BlockSpec((1, 2 * half, s), lambda i: (i, 0, 0)),
        out_shape=jax.ShapeDtypeStruct((b, 2 * half, s), F32),
        compiler_params=_cparams(("parallel",)),
        name="rope_tables",
    )(positions.astype(F32).reshape(b, 1, s), inv_freq.reshape(half, 1))


def _ffn_apply(x, g_ref, wg_ref, wu_ref, wd_ref):
    n = _rms(x, g_ref[...]).astype(BF16)
    acc = jnp.zeros_like(x)
    for j in range(wg_ref.shape[1] // FF_CHUNK):
        cols = slice(j * FF_CHUNK, (j + 1) * FF_CHUNK)
        gate = _dot(n, wg_ref[:, cols])
        up = _dot(n, wu_ref[:, cols])
        act = (gate * jax.nn.sigmoid(gate) * up).astype(BF16)
        acc = acc + _dot(act, wd_ref[cols, :])
    return x + FFN_RES * acc


def _ffn_body(x_ref, g_ref, wg_ref, wu_ref, wd_ref, o_ref):
    o_ref[...] = _ffn_apply(x_ref[...], g_ref, wg_ref, wu_ref, wd_ref)


def _resident(shape):
    return pl.BlockSpec(shape, lambda *_: (0,) * len(shape), pipeline_mode=pl.Buffered(1))


def _ffn_operands(norm, w_gate, w_up, w_down):
    d, d_ff = w_gate.shape
    assert d_ff % FF_CHUNK == 0
    args = [norm.reshape(1, d).astype(F32), w_gate.astype(BF16), w_up.astype(BF16), w_down.astype(BF16)]
    return args, [_resident(a.shape) for a in args]


def _ffn(x2, norm, w_gate, w_up, w_down):
    t, d = x2.shape
    tm = min(TOKEN_TILE, t)
    row = pl.BlockSpec((tm, d), lambda i: (i, 0))
    args, specs = _ffn_operands(norm, w_gate, w_up, w_down)
    return pl.pallas_call(
        _ffn_body,
        grid=(t // tm,),
        in_specs=[row] + specs,
        out_specs=row,
        out_shape=jax.ShapeDtypeStruct((t, d), F32),
        compiler_params=_cparams(("parallel",)),
        name="ffn",
    )(x2, *args)


def _inproj_body(h_ref, g_ref, win_ref, qn_ref, kvn_ref, wq_ref, wk_ref, wvt_ref,
                 rope_ref, place_ref, base_ref, one_ref,
                 q_ref, k_ref, vt_ref, u_ref, ga_ref, gs_ref, *, dims):
    q_rank, kv_rank, ssm_w, d_model, scale = dims
    n = _rms(h_ref[0], g_ref[...]).astype(BF16)
    z = _dot(n, win_ref[...])
    o = 0
    c_q = z[:, o:o + q_rank]; o += q_rank
    c_kv = z[:, o:o + kv_rank]; o += kv_rank
    kr = z[:, o:o + HEAD_PAD]; o += HEAD_PAD
    u = z[:, o:o + ssm_w]; o += ssm_w
    g_attn = z[:, o:o + d_model]; o += d_model
    g_ssm = z[:, o:o + d_model]

    u_ref[0] = u.astype(BF16)
    ga_ref[0] = jax.nn.sigmoid(g_attn).astype(BF16)
    gs_ref[0] = jax.nn.sigmoid(g_ssm).astype(BF16)

    cos_sin = _dot_tn(_bf16_parts(rope_ref[0]), place_ref[...])
    cos = cos_sin[:, :HEAD_PAD] + base_ref[...]
    sin = cos_sin[:, HEAD_PAD:]

    cqn = _rms(c_q, qn_ref[...]).astype(BF16)
    q_t = _dot_nt(wq_ref[...], cqn)
    ckvn = _rms(c_kv, kvn_ref[...]).astype(BF16)
    k_nope = _dot(ckvn, wk_ref[...])
    k_rope = kr * cos + pltpu.roll(kr, HEAD_PAD - QK_ROPE_DIM, axis=1) * sin
    k_rope = k_rope + one_ref[...]
    half = rope_ref.shape[1] // 2
    cos_r, sin_r = rope_ref[0, :half, :], rope_ref[0, half:, :]
    tokens = cos_r.shape[1]
    cos_t = jnp.concatenate([jnp.ones((QK_NOPE_DIM, tokens), F32), cos_r, cos_r,
                             jnp.zeros((QK_ROPE_DIM, tokens), F32)], axis=0)
    sin_t = jnp.concatenate([jnp.zeros((QK_NOPE_DIM, tokens), F32), sin_r, sin_r,
                             jnp.zeros((QK_ROPE_DIM, tokens), F32)], axis=0)
    for h in range(MLA_HEADS):
        sl = slice(h * HEAD_PAD, (h + 1) * HEAD_PAD)
        blk = q_t[sl]
        partner = jnp.concatenate([blk[QK_ROPE_DIM:], blk[:QK_ROPE_DIM]], axis=0)
        q_ref[0, h] = ((blk * cos_t + partner * sin_t) * scale).astype(BF16)
        k_ref[0, h] = (k_nope[:, sl] + k_rope).astype(BF16)
    vt = _dot_nt(wvt_ref[...], ckvn).astype(BF16)
    ones = jnp.ones((V_ONES_ROWS, vt.shape[1]), BF16)
    vt_ref[0, 0] = jnp.concatenate(
        [piece for h in range(MLA_HEADS) for piece in (vt[h * V_HEAD_DIM:(h + 1) * V_HEAD_DIM], ones)], axis=0)


def _place_cols(w, starts, width, total):
    out = jnp.zeros((w.shape[0], total), w.dtype)
    for i, s in enumerate(starts):
        out = lax.dynamic_update_slice(out, w[:, i * width:(i + 1) * width], (0, s))
    return out


def _rot_half(w):
    half = w.shape[1] // 2
    return jnp.concatenate([-w[:, half:], w[:, :half]], axis=1)


def _inproj(h1, rope_t, mix_norm, w_in, q_norm, w_uq, kv_norm, w_ukv, ssm_w):
    b, s, d = h1.shape
    q_rank, kv_rank = q_norm.shape[0], kv_norm.shape[0]
    hq = QK_NOPE_DIM + QK_ROPE_DIM
    hkv = QK_NOPE_DIM + V_HEAD_DIM
    scale = float(hq) ** -0.5 * math.log2(math.e)
    o_kr = q_rank + kv_rank
    o_u = o_kr + QK_ROPE_DIM
    w_kr = w_in[:, o_kr:o_u]
    assert hq + QK_ROPE_DIM == HEAD_PAD
    kr_p = jnp.concatenate([jnp.zeros((d, QK_NOPE_DIM), w_in.dtype), w_kr, _rot_half(w_kr)], axis=1)
    win = jnp.concatenate([w_in[:, :o_kr], kr_p, w_in[:, o_u:]], axis=1).astype(BF16)
    wq = jnp.concatenate(
        [piece for h in range(MLA_HEADS)
         for piece in (w_uq[:, h * hq:(h + 1) * hq], _rot_half(w_uq[:, h * hq + QK_NOPE_DIM:(h + 1) * hq]))],
        axis=1).T.astype(BF16)
    wk = jnp.concatenate([jnp.pad(w_ukv[:, h * hkv:h * hkv + QK_NOPE_DIM], ((0, 0), (0, HEAD_PAD - QK_NOPE_DIM)))
                          for h in range(MLA_HEADS)], axis=1).astype(BF16)
    wvt = jnp.concatenate([w_ukv[:, h * hkv + QK_NOPE_DIM:(h + 1) * hkv] for h in range(MLA_HEADS)],
                          axis=1).T.astype(BF16)
    half = QK_ROPE_DIM // 2
    eye = jnp.eye(half, dtype=F32)
    place1 = _place_cols(jnp.concatenate([eye, eye], axis=1), [QK_NOPE_DIM], QK_ROPE_DIM, HEAD_PAD)
    zero = jnp.zeros_like(place1)
    place = jnp.concatenate([jnp.concatenate([place1, zero], axis=1),
                             jnp.concatenate([zero, place1], axis=1)], axis=0)
    place = jnp.tile(place, (3, 1)).astype(BF16)
    base = (jnp.arange(HEAD_PAD) < QK_NOPE_DIM).astype(F32).reshape(1, HEAD_PAD)

    tm = min(TOKEN_TILE, s)
    nt = s // tm
    hv = MLA_HEADS * (V_HEAD_DIM + V_ONES_ROWS)
    const = lambda shape: pl.BlockSpec(shape, lambda i, j: (0,) * len(shape))
    tok = lambda w: pl.BlockSpec((1, tm, w), lambda i, j: (i, j, 0))
    heads = pl.BlockSpec((1, MLA_HEADS, tm, HEAD_PAD), lambda i, j: (i, 0, j, 0))
    heads_t = pl.BlockSpec((1, MLA_HEADS, HEAD_PAD, tm), lambda i, j: (i, 0, 0, j))
    one = (jnp.arange(HEAD_PAD) == AUG_LANE).astype(F32).reshape(1, HEAD_PAD)
    rope = pl.BlockSpec((1, 2 * half, tm), lambda i, j: (i, 0, j))
    dims = (q_rank, kv_rank, ssm_w, d, scale)
    return pl.pallas_call(
        functools.partial(_inproj_body, dims=dims),
        grid=(b, nt),
        in_specs=[tok(d), const((1, d)), const(win.shape), const((1, q_rank)), const((1, kv_rank)),
                  const(wq.shape), const(wk.shape), const(wvt.shape),
                  rope, const(place.shape), const(base.shape), const(one.shape)],
        out_specs=[heads_t, heads, pl.BlockSpec((1, 1, hv, tm), lambda i, j: (i, j, 0, 0)),
                   tok(ssm_w), tok(d), tok(d)],
        out_shape=[jax.ShapeDtypeStruct((b, MLA_HEADS, HEAD_PAD, s), BF16),
                   jax.ShapeDtypeStruct((b, MLA_HEADS, s, HEAD_PAD), BF16),
                   jax.ShapeDtypeStruct((b, nt, hv, tm), BF16),
                   jax.ShapeDtypeStruct((b, s, ssm_w), BF16),
                   jax.ShapeDtypeStruct((b, s, d), BF16),
                   jax.ShapeDtypeStruct((b, s, d), BF16)],
        compiler_params=_cparams(("parallel", "parallel")),
        name="inproj",
    )(h1, mix_norm.reshape(1, d).astype(F32), win, q_norm.reshape(1, q_rank).astype(F32),
      kv_norm.reshape(1, kv_rank).astype(F32), wq, wk, wvt, rope_t, place, base, one)


def _attn_body(q_ref, k_ref, vt_ref, o_ref, s_ref, acc_ref, *, n_kt, tk):
    qt = q_ref[0, 0]
    tq = qt.shape[1]
    pack = 2 * SUBLANES
    first = lax.broadcasted_iota(jnp.int32, (pack, tq), 0) == 0

    def with_reference(m):
        row = jnp.where(first, jnp.broadcast_to(-m, (pack, tq)), 0.0).astype(BF16)
        return jnp.concatenate([qt[:AUG_LANE], row, qt[AUG_LANE + pack:]], axis=0)

    def rounded(m):
        return m.astype(BF16).astype(F32)

    def finish():
        acc = acc_ref[...]
        o_ref[0] = (acc[:V_HEAD_DIM] / acc[V_HEAD_DIM:V_HEAD_DIM + 1]).astype(BF16)

    s = _dot(k_ref[0, 0, 0:tk, :], qt)
    m = rounded(jnp.max(s, axis=0, keepdims=True))
    p_prev = jnp.exp2(s - m).astype(BF16)
    beta = jnp.ones_like(m)
    acc_ref[...] = jnp.zeros_like(acc_ref)
    gap = jnp.zeros_like(m)
    for j in range(1, n_kt):
        p = jnp.exp2(_dot(k_ref[0, 0, j * tk:(j + 1) * tk, :], with_reference(m))).astype(BF16)
        acc_ref[...] = (acc_ref[...] + _dot(vt_ref[0, j - 1], p_prev)) * beta
        g = jnp.log2(jnp.max(p, axis=0, keepdims=True).astype(F32))
        gap = jnp.maximum(gap, g)
        m_new = rounded(m + jnp.maximum(g, 0.0))
        beta = jnp.exp2(m - m_new)
        m = m_new
        p_prev = p
    acc_ref[...] = (acc_ref[...] + _dot(vt_ref[0, n_kt - 1], p_prev)) * beta
    finish()

    @pl.when(jnp.logical_not(jnp.max(gap) <= ATTN_MAX_GAP))
    def _():
        def tile(j, m):
            start = pl.multiple_of(j * tk, tk)
            s_ref[...] = _dot(k_ref[0, 0, pl.ds(start, tk), :], qt)
            s = s_ref[...]
            m_new = jnp.maximum(m, jnp.max(s, axis=0, keepdims=True))
            p = jnp.exp2(s - m_new).astype(BF16)
            acc_ref[...] = jnp.exp2(m - m_new) * acc_ref[...] + _dot(vt_ref[0, j], p)
            return m_new

        acc_ref[...] = jnp.zeros_like(acc_ref)
        lax.fori_loop(0, n_kt, tile, jnp.full((1, tq), -jnp.inf, F32))
        finish()


def _attention(q_t, k, vt):
    b, h, s, _ = k.shape
    n_kt, tk = vt.shape[1], vt.shape[3]
    vrows = vt.shape[2] // h
    tq = min(ATTN_TQ, s)
    return pl.pallas_call(
        functools.partial(_attn_body, n_kt=n_kt, tk=tk),
        grid=(b, h, s // tq),
        in_specs=[pl.BlockSpec((1, 1, HEAD_PAD, tq), lambda i, j, t: (i, j, 0, t)),
                  pl.BlockSpec((1, 1, s, HEAD_PAD), lambda i, j, t: (i, j, 0, 0)),
                  pl.BlockSpec((1, n_kt, vrows, tk), lambda i, j, t: (i, 0, j, 0))],
        out_specs=pl.BlockSpec((1, V_HEAD_DIM, tq), lambda i, j, t: (i, j, t)),
        out_shape=jax.ShapeDtypeStruct((b, h * V_HEAD_DIM, s), BF16),
        scratch_shapes=[pltpu.VMEM((tk, tq), F32), pltpu.VMEM((vrows, tq), F32)],
        compiler_params=_cparams(("parallel", "parallel", "arbitrary")),
        name="attention",
    )(q_t, k, vt)


def _discretise(lr, li, ldt):
    dt = jnp.exp(ldt)
    mag = jnp.exp(lr * dt)
    lb_re = mag * jnp.cos(li * dt)
    lb_im = mag * jnp.sin(li * dt)
    den = lr * lr + li * li
    nr = lb_re - 1.0
    return (nr * lr + lb_im * li) / den, (lb_im * lr - nr * li) / den, lr * dt, li * dt


def _ssm_prep_body(rowp_ref, colp_ref, bt_re_ref, bt_im_ref, bc_re_ref, bc_im_ref, cr_re_ref, cr_im_ref, dt_ref,
                   tt_ref, pp_ref, qq_ref, dec_ref, *, n_state):
    L = SSM_CHUNK
    rows = tt_ref.shape[1]
    width = rows // L
    lanes2 = 2 * n_state
    parity = pl.program_id(0) % 2
    lane = lax.broadcasted_iota(jnp.int32, (1, lanes2), 1)
    keep_lane = (lane >= parity * n_state) & (lane < (parity + 1) * n_state)
    first_lane = lane < n_state
    n_pow = (L + 1 + SUBLANES - 1) // SUBLANES * SUBLANES
    kpow = lax.broadcasted_iota(jnp.int32, (n_pow, 1), 0).astype(F32)
    rep = (lax.broadcasted_iota(jnp.int32, (rows + width, n_pow), 0) // width
           == lax.broadcasted_iota(jnp.int32, (rows + width, n_pow), 1)).astype(F32)
    sl = lax.broadcasted_iota(jnp.int32, (1, rows), 1) // width
    blk = lambda a, k: a[k * width:(k + 1) * width]

    k_rows = []
    for d in range(2):
        k_re, k_im, ar, ai = _discretise(rowp_ref[0, d, 0:1, :], rowp_ref[0, d, 1:2, :], rowp_ref[0, d, 2:3, :])
        magk = jnp.exp(kpow * ar)
        ak_re = _dot_exact(rep, magk * jnp.cos(kpow * ai))
        ak_im = _dot_exact(rep, magk * jnp.sin(kpow * ai))
        bt_re, bt_im = bt_re_ref[0, d], bt_im_ref[0, d]
        bb_re = k_re * bt_re - k_im * bt_im
        bb_im = k_re * bt_im + k_im * bt_re
        ba_re = bb_re * ak_re[:rows] - bb_im * ak_im[:rows]
        ba_im = bb_re * ak_im[:rows] + bb_im * ak_re[:rows]
        blocks = range(L - 1, -1, -1) if d == 0 else range(L)
        p_re = jnp.concatenate([blk(ba_re, k) for k in blocks], axis=0)
        p_im = jnp.concatenate([blk(ba_im, k) for k in blocks], axis=0)
        pp_ref[0, :, (2 * d) * lanes2:(2 * d + 1) * lanes2] = jnp.where(keep_lane, p_re, 0.0).astype(BF16)
        pp_ref[0, :, (2 * d + 1) * lanes2:(2 * d + 2) * lanes2] = jnp.where(keep_lane, p_im, 0.0).astype(BF16)
        cr_re, cr_im = cr_re_ref[0, d], cr_im_ref[0, d]
        ca_re = cr_re * ak_re - cr_im * ak_im
        ca_im = cr_re * ak_im + cr_im * ak_re
        blocks = range(1, L + 1) if d == 0 else range(L, 0, -1)
        q_re = jnp.concatenate([blk(ca_re, k) for k in blocks], axis=0)
        q_im = jnp.concatenate([blk(ca_im, k) for k in blocks], axis=0)
        qq_ref[0, :, (2 * d) * lanes2:(2 * d + 1) * lanes2] = jnp.where(keep_lane, q_re, 0.0).astype(BF16)
        qq_ref[0, :, (2 * d + 1) * lanes2:(2 * d + 2) * lanes2] = jnp.where(keep_lane, -q_im, 0.0).astype(BF16)
        kc_re, kc_im, _, _ = _discretise(colp_ref[0, d, :, 0:1], colp_ref[0, d, :, 1:2], colp_ref[0, d, :, 2:3])
        bc_re, bc_im = bc_re_ref[0, d], bc_im_ref[0, d]
        bbc_re = kc_re * bc_re - kc_im * bc_im
        bbc_im = kc_re * bc_im + kc_im * bc_re
        k_rows.append(_dot_exact(jnp.where(first_lane, ca_re[:rows], 0.0), bbc_re)
                      - _dot_exact(jnp.where(first_lane, ca_im[:rows], 0.0), bbc_im))
        dec_ref[0, 2 * d:2 * d + 1, :] = jnp.where(keep_lane, ak_re[rows:rows + 1], 0.0)
        dec_ref[0, 2 * d + 1:2 * d + 2, :] = jnp.where(keep_lane, ak_im[rows:rows + 1], 0.0)
    dec_ref[0, 4:8, :] = jnp.zeros((4, lanes2), F32)

    kf, kb = k_rows
    diag = blk(kf, 0) + blk(kb, 0) + dt_ref[0]
    for t in range(L):
        acc = jnp.where(sl == t, diag, 0.0)
        for k in range(1, t + 1):
            acc = jnp.where(sl == t - k, blk(kf, k), acc)
        for k in range(1, L - t):
            acc = jnp.where(sl == t + k, blk(kb, k), acc)
        tt_ref[0, t * width:(t + 1) * width, :] = acc.astype(BF16)


def _ssm_prep(lam_re, lam_im, log_dt, b_re, b_im, c_re, c_im, d_skip):
    _, g, p, w = b_re.shape
    L = SSM_CHUNK
    rows = L * w
    dup = lambda a, axis: jnp.concatenate([a, a], axis=axis)
    ldt = jnp.broadcast_to(log_dt[:, :, None], lam_re.shape)
    params = [dup(lam_re, -1), dup(lam_im, -1), dup(ldt, -1)]
    rowp = jnp.pad(jnp.stack(params, axis=2), ((0, 0), (0, 0), (0, SUBLANES - 3), (0, 0)))
    colp = jnp.pad(jnp.stack(params, axis=3), ((0, 0), (0, 0), (0, 0), (0, LANES - 3)))
    per_group = lambda a: jnp.swapaxes(a, 0, 1).astype(F32)
    bt = lambda b: jnp.tile(dup(jnp.swapaxes(b, 2, 3), -1), (1, 1, L, 1))
    bc = lambda b: jnp.tile(dup(b, 2), (1, 1, 1, L))
    cr = lambda c: jnp.tile(dup(c, -1), (1, 1, L + 1, 1))
    dmat = jnp.where(jnp.eye(w, dtype=bool)[None], d_skip.reshape(g, 1, w), 0.0)
    dtile = jnp.tile(dmat, (1, 1, L)).astype(F32)
    grp = lambda shape: pl.BlockSpec((1,) + shape, lambda i: (i,) + (0,) * len(shape))
    return pl.pallas_call(
        functools.partial(_ssm_prep_body, n_state=p),
        grid=(g,),
        in_specs=[grp((2, SUBLANES, 2 * p)), grp((2, 2 * p, LANES)), grp((2, rows, 2 * p)), grp((2, rows, 2 * p)),
                  grp((2, 2 * p, rows)), grp((2, 2 * p, rows)), grp((2, rows + w, 2 * p)), grp((2, rows + w, 2 * p)),
                  grp((w, rows))],
        out_specs=[grp((rows, rows)), grp((rows, 8 * p)), grp((rows, 8 * p)), grp((SUBLANES, 2 * p))],
        out_shape=[jax.ShapeDtypeStruct((g, rows, rows), BF16),
                   jax.ShapeDtypeStruct((g, rows, 8 * p), BF16),
                   jax.ShapeDtypeStruct((g, rows, 8 * p), BF16),
                   jax.ShapeDtypeStruct((g, SUBLANES, 2 * p), F32)],
        compiler_params=_cparams(("parallel",)),
        name="ssm_prep",
    )(per_group(rowp), per_group(colp), per_group(bt(b_re)), per_group(bt(b_im)), per_group(bc(b_re)),
      per_group(bc(b_im)), per_group(cr(c_re)), per_group(cr(c_im)), dtile)


def _utile_body(u_ref, eye_ref, o_ref, scr_ref):
    scr_ref[...] = u_ref[0].astype(F32)
    half = SSM_CHUNK // 2
    for sh in range(2):
        rows = jnp.concatenate([scr_ref[:, sh * half + s, :] for s in range(half)], axis=0).astype(BF16)
        o_ref[0, sh] = _dot_nt(eye_ref[...], rows).astype(BF16)


def _utile(u):
    b, s, w = u.shape
    L = SSM_CHUNK
    n_cb = s // (L * LANES)
    u4 = u.reshape(b, s // L, L, w)
    eye = jnp.eye(w, dtype=BF16)
    return pl.pallas_call(
        _utile_body,
        grid=(b, n_cb),
        in_specs=[pl.BlockSpec((1, LANES, L, w), lambda i, j: (i, j, 0, 0)),
                  pl.BlockSpec((w, w), lambda i, j: (0, 0))],
        out_specs=pl.BlockSpec((1, 2, w, LANES * L // 2), lambda i, j: (i, j, 0, 0)),
        out_shape=jax.ShapeDtypeStruct((b, 2 * n_cb, w, LANES * L // 2), BF16),
        scratch_shapes=[pltpu.VMEM((LANES, L, w), F32)],
        compiler_params=_cparams(("parallel", "parallel")),
        name="ssm_utile",
    )(u4, eye)


def _ssm_body(u_ref, tt_ref, pp_ref, qq_ref, dec_ref, y_ref, st_ref, *, n_chunks):
    bsz = u_ref.shape[0]
    n_cb = u_ref.shape[1] // 2
    width = u_ref.shape[2] // 2
    L = SSM_CHUNK
    half = L // 2
    w = LANES
    assert bsz == SUBLANES

    def tile_of(cb, step):
        return 2 * cb + step // half, slice((step % half) * w, (step % half + 1) * w)

    def ucol(b, g):
        pieces = []
        for s in range(L):
            row = []
            for cb in range(n_cb):
                tile, lanes = tile_of(cb, s)
                row.append(u_ref[b, tile, g * width:(g + 1) * width, lanes])
            pieces.append(jnp.concatenate(row, axis=1))
        return jnp.concatenate(pieces, axis=0)

    for b in range(bsz):
        e = _dot_tn(ucol(b, 0), pp_ref[0]) + _dot_tn(ucol(b, 1), pp_ref[1])
        for i in range(4):
            st_ref[i, pl.ds(b, n_chunks, stride=bsz), :] = e[:, i * w:(i + 1) * w]

    dec = dec_ref[0] + dec_ref[1]
    a_fr, a_fi, a_br, a_bi = (jnp.broadcast_to(dec[i:i + 1, :], (bsz, w)) for i in range(4))

    def step(c, carry):
        hfr, hfi, hbr, hbi = carry
        rf = pl.multiple_of(c * bsz, bsz)
        rb = pl.multiple_of((n_chunks - 1 - c) * bsz, bsz)
        efr = st_ref[0, pl.ds(rf, bsz), :]
        efi = st_ref[1, pl.ds(rf, bsz), :]
        ebr = st_ref[2, pl.ds(rb, bsz), :]
        ebi = st_ref[3, pl.ds(rb, bsz), :]
        st_ref[0, pl.ds(rf, bsz), :] = hfr
        st_ref[1, pl.ds(rf, bsz), :] = hfi
        st_ref[2, pl.ds(rb, bsz), :] = hbr
        st_ref[3, pl.ds(rb, bsz), :] = hbi
        return (a_fr * hfr - a_fi * hfi + efr, a_fi * hfr + a_fr * hfi + efi,
                a_br * hbr - a_bi * hbi + ebr, a_bi * hbr + a_br * hbi + ebi)

    zero = jnp.zeros((bsz, w), F32)
    lax.fori_loop(0, n_chunks, step, (zero, zero, zero, zero))

    for b in range(bsz):
        h = jnp.concatenate([st_ref[i, pl.ds(b, n_chunks, stride=bsz), :] for i in range(4)],
                            axis=1).astype(BF16)
        for g in range(2):
            y = (_dot(tt_ref[g], ucol(b, g)) + _dot_nt(qq_ref[g], h)).astype(BF16)
            for t in range(L):
                for cb in range(n_cb):
                    tile, lanes = tile_of(cb, t)
                    y_ref[b, tile, g * width:(g + 1) * width, lanes] = y[t * width:(t + 1) * width, cb * w:(cb + 1) * w]


def _ssm(u_t, tt, pp, qq, dec):
    b, n_tiles, ssm_w, tl = u_t.shape
    g, k, _ = tt.shape
    n_state8 = pp.shape[2]
    width = k // SSM_CHUNK
    n_chunks = n_tiles // 2 * LANES
    pair = lambda shape: pl.BlockSpec((2,) + shape, lambda i: (i,) + (0,) * len(shape))
    seq = pl.BlockSpec((b, n_tiles, 2 * width, tl), lambda i: (0, 0, i, 0))
    return pl.pallas_call(
        functools.partial(_ssm_body, n_chunks=n_chunks),
        grid=(g // 2,),
        in_specs=[seq, pair((k, k)), pair((k, n_state8)), pair((k, n_state8)), pair((SUBLANES, dec.shape[2]))],
        out_specs=seq,
        out_shape=jax.ShapeDtypeStruct(u_t.shape, BF16),
        scratch_shapes=[pltpu.VMEM((n_state8 // LANES, n_chunks * b, LANES), F32)],
        compiler_params=_cparams(("parallel",)),
        name="ssm",
    )(u_t, tt, pp, qq, dec)


def _ssm_post_body(y_ref, wglu_ref, bglu_ref, wos_ref, o_ref):
    ssm_w = y_ref.shape[2]
    y = y_ref[0, 0].astype(F32)
    glu = _dot_tn(jax.nn.gelu(y).astype(BF16), wglu_ref[...]) + bglu_ref[...]
    ssm_out = glu[:, :ssm_w] * jax.nn.sigmoid(glu[:, ssm_w:])
    out = _dot(ssm_out.astype(BF16), wos_ref[...])
    for s in range(SSM_CHUNK // 2):
        o_ref[0, :, s, :] = out[s * LANES:(s + 1) * LANES]


def _ssm_post(y_t, w_glu, b_glu, w_o_ssm):
    b, n_tiles, ssm_w, tl = y_t.shape
    d = w_o_ssm.shape[1]
    L = SSM_CHUNK
    s = n_tiles * tl
    const = lambda shape: pl.BlockSpec(shape, lambda i, j: (0,) * len(shape))
    out = pl.pallas_call(
        _ssm_post_body,
        grid=(b, n_tiles),
        in_specs=[pl.BlockSpec((1, 1, ssm_w, tl), lambda i, j: (i, j, 0, 0)),
                  const(w_glu.shape), const((1, w_glu.shape[1])), const(w_o_ssm.shape)],
        out_specs=pl.BlockSpec((1, LANES, L // 2, d), lambda i, j: (i, j // 2, j % 2, 0)),
        out_shape=jax.ShapeDtypeStruct((b, s // L, L, d), F32),
        compiler_params=_cparams(("parallel", "parallel")),
        name="ssm_post",
    )(y_t, w_glu.astype(BF16), b_glu.reshape(1, -1).astype(F32), w_o_ssm.astype(BF16))
    return out.reshape(b, s, d)


def _merge_ffn_body(h_ref, at_ref, ssm_ref, ga_ref, gs_ref, woa_ref, wout_ref, g_ref, wg_ref, wu_ref, wd_ref,
                    *rest):
    o_ref = rest[-1]
    attn = _dot_tn(at_ref[0], woa_ref[...])
    merged = ga_ref[0].astype(F32) * attn + gs_ref[0].astype(F32) * ssm_ref[0]
    h = h_ref[0] + _dot(merged.astype(BF16), wout_ref[...])
    h = _ffn_apply(h, g_ref, wg_ref, wu_ref, wd_ref)
    if len(rest) == 2:
        h = _rms(h, rest[0][...])
    o_ref[0] = h


def _merge_ffn(h1, attn_t, ssm, ga, gs, w_o_attn, w_out, norm, w_gate, w_up, w_down, final_norm=None):
    b, s, d = h1.shape
    hv = attn_t.shape[1]
    tm = min(TOKEN_TILE, s)
    tok = lambda w: pl.BlockSpec((1, tm, w), lambda i, j: (i, j, 0))
    ffn_args, ffn_specs = _ffn_operands(norm, w_gate, w_up, w_down)
    args = [h1, attn_t, ssm, ga, gs, w_o_attn.astype(BF16), w_out.astype(BF16)] + ffn_args
    in_specs = [tok(d), pl.BlockSpec((1, hv, tm), lambda i, j: (i, 0, j)), tok(d), tok(d), tok(d),
                _resident(w_o_attn.shape), _resident(w_out.shape)] + ffn_specs
    if final_norm is not None:
        args.append(final_norm.reshape(1, d).astype(F32))
        in_specs.append(_resident((1, d)))
    return pl.pallas_call(
        _merge_ffn_body,
        grid=(b, s // tm),
        in_specs=in_specs,
        out_specs=tok(d),
        out_shape=jax.ShapeDtypeStruct((b, s, d), F32),
        compiler_params=_cparams(("parallel", "parallel")),
        name="merge_ffn",
    )(*args)


def kernel(x, positions, ffn1_norm, ffn1_w_gate, ffn1_w_up, ffn1_w_down, mix_norm, w_in, q_norm, w_uq, kv_norm, w_ukv, w_o_attn, ssm_lambda_re, ssm_lambda_im, ssm_log_dt, ssm_b_re, ssm_b_im, ssm_c_re, ssm_c_im, ssm_d, w_glu, b_glu, w_o_ssm, w_out, ffn2_norm, ffn2_w_gate, ffn2_w_up, ffn2_w_down, final_norm):
    b, s, d = x.shape
    depth = ffn1_norm.shape[0]
    _, _, n_grp, _, grp_w = ssm_b_re.shape
    ssm_w = n_grp * grp_w
    L = SSM_CHUNK
    assert depth >= 1, "the final norm is fused into the last layer's second FFN"
    assert b == SUBLANES, "the S5 chunk recurrence keeps the batch on the 8 sublanes"
    assert s % max(TOKEN_TILE, ATTN_TQ, L * LANES) == 0 and L * grp_w == MXU_DIM and n_grp % 2 == 0

    rope_t = _rope_tables(positions)
    h = x
    for l in range(depth):
        last = l == depth - 1
        h = _ffn(h.reshape(b * s, d), ffn1_norm[l], ffn1_w_gate[l], ffn1_w_up[l], ffn1_w_down[l]).reshape(b, s, d)
        q, k, vt, u, ga, gs = _inproj(h, rope_t, mix_norm[l], w_in[l], q_norm[l], w_uq[l],
                                      kv_norm[l], w_ukv[l], ssm_w)
        attn_t = _attention(q, k, vt)

        tt, pp, qq, dec = _ssm_prep(ssm_lambda_re[l], ssm_lambda_im[l], ssm_log_dt[l], ssm_b_re[l], ssm_b_im[l],
                                    ssm_c_re[l], ssm_c_im[l], ssm_d[l])
        y_t = _ssm(_utile(u), tt, pp, qq, dec)
        ssm = _ssm_post(y_t, w_glu[l], b_glu[l], w_o_ssm[l])

        h = _merge_ffn(h, attn_t, ssm, ga, gs, w_o_attn[l], w_out[l], ffn2_norm[l], ffn2_w_gate[l],
                       ffn2_w_up[l], ffn2_w_down[l], final_norm if last else None)
    return h
```

```python
import functools
import math

import jax
import jax.numpy as jnp
from jax import lax
from jax.experimental import pallas as pl
from jax.experimental.pallas import tpu as pltpu

F32 = jnp.float32
BF16 = jnp.bfloat16

MLA_HEADS = 8
QK_NOPE_DIM = 64
QK_ROPE_DIM = 32
V_HEAD_DIM = 64
ROPE_THETA = 10000.0
FFN_RES = 0.5
EPS = 1e-6

LANES = 128
SUBLANES = 8
MXU_DIM = 256
VMEM_LIMIT_BYTES = 56 * 1024 * 1024

HEAD_PAD = LANES
AUG_LANE = 96
V_ONES_ROWS = 16
SSM_CHUNK = 16

TOKEN_TILE = 512
ATTN_TQ = 2048
ATTN_SPLIT = 512
FF_CHUNK = 256
ATTN_MAX_GAP = 64.0

def _cparams(sem, flags=None):
    return pltpu.CompilerParams(dimension_semantics=sem, vmem_limit_bytes=VMEM_LIMIT_BYTES, flags=flags)


def _rms(x, g):
    return x * lax.rsqrt(jnp.mean(x * x, axis=-1, keepdims=True) + EPS) * g


def _dot(a, b):
    return jnp.dot(a, b, preferred_element_type=F32)


def _dot_nt(a, b):
    return lax.dot_general(a, b, (((1,), (1,)), ((), ())), preferred_element_type=F32)


def _dot_tn(a, b):
    return lax.dot_general(a, b, (((0,), (0,)), ((), ())), preferred_element_type=F32)


def _dot_exact(a, b):
    return jnp.dot(a, b, preferred_element_type=F32, precision=lax.Precision.HIGHEST)


def _bf16_parts(a):
    hi = a.astype(BF16)
    rest = a - hi.astype(F32)
    mid = rest.astype(BF16)
    lo = (rest - mid.astype(F32)).astype(BF16)
    return jnp.concatenate([hi, mid, lo], axis=0)


def _rope_body(pos_ref, invf_ref, o_ref):
    half = invf_ref.shape[0]
    ang = invf_ref[...] * pos_ref[0]
    o_ref[0, :half, :] = jnp.cos(ang)
    o_ref[0, half:, :] = jnp.sin(ang)


def _rope_tables(positions):
    b, s = positions.shape
    half = QK_ROPE_DIM // 2
    inv_freq = ROPE_THETA ** (-jnp.arange(0, QK_ROPE_DIM, 2, dtype=F32) / QK_ROPE_DIM)
    return pl.pallas_call(
        _rope_body,
        grid=(b,),
        in_specs=[pl.BlockSpec((1, 1, s), lambda i: (i, 0, 0)), pl.BlockSpec((half, 1), lambda i: (0, 0))],
        out_specs=pl.BlockSpec((1, 2 * half, s), lambda i: (i, 0, 0)),
        out_shape=jax.ShapeDtypeStruct((b, 2 * half, s), F32),
        compiler_params=_cparams(("parallel",)),
        name="rope_tables",
    )(positions.astype(F32).reshape(b, 1, s), inv_freq.reshape(half, 1))


def _ffn_apply(x, g_ref, wg_ref, wu_ref, wd_ref):
    n = _rms(x, g_ref[...]).astype(BF16)
    acc = jnp.zeros_like(x)
    for j in range(wg_ref.shape[1] // FF_CHUNK):
        cols = slice(j * FF_CHUNK, (j + 1) * FF_CHUNK)
        gate = _dot(n, wg_ref[:, cols])
        up = _dot(n, wu_ref[:, cols])
        act = (gate * jax.nn.sigmoid(gate) * up).astype(BF16)
        acc = acc + _dot(act, wd_ref[cols, :])
    return x + FFN_RES * acc


def _ffn_body(x_ref, g_ref, wg_ref, wu_ref, wd_ref, o_ref):
    o_ref[...] = _ffn_apply(x_ref[...], g_ref, wg_ref, wu_ref, wd_ref)


def _resident(shape):
    return pl.BlockSpec(shape, lambda *_: (0,) * len(shape), pipeline_mode=pl.Buffered(1))


def _ffn_operands(norm, w_gate, w_up, w_down):
    d, d_ff = w_gate.shape
    assert d_ff % FF_CHUNK == 0
    args = [norm.reshape(1, d).astype(F32), w_gate.astype(BF16), w_up.astype(BF16), w_down.astype(BF16)]
    return args, [_resident(a.shape) for a in args]


def _ffn(x2, norm, w_gate, w_up, w_down):
    t, d = x2.shape
    tm = min(TOKEN_TILE, t)
    row = pl.BlockSpec((tm, d), lambda i: (i, 0))
    args, specs = _ffn_operands(norm, w_gate, w_up, w_down)
    return pl.pallas_call(
        _ffn_body,
        grid=(t // tm,),
        in_specs=[row] + specs,
        out_specs=row,
        out_shape=jax.ShapeDtypeStruct((t, d), F32),
        compiler_params=_cparams(("parallel",)),
        name="ffn",
    )(x2, *args)


def _inproj_body(h_ref, g_ref, win_ref, qn_ref, kvn_ref, wq_ref, wk_ref, wvt_ref,
                 rope_ref, place_ref, base_ref, one_ref,
                 q_ref, k_ref, vt_ref, u_ref, ga_ref, gs_ref, *, dims):
    q_rank, kv_rank, ssm_w, d_model, scale = dims
    n = _rms(h_ref[0], g_ref[...]).astype(BF16)
    z = _dot(n, win_ref[...])
    o = 0
    c_q = z[:, o:o + q_rank]; o += q_rank
    c_kv = z[:, o:o + kv_rank]; o += kv_rank
    kr = z[:, o:o + HEAD_PAD]; o += HEAD_PAD
    u = z[:, o:o + ssm_w]; o += ssm_w
    g_attn = z[:, o:o + d_model]; o += d_model
    g_ssm = z[:, o:o + d_model]

    u_ref[0] = u.astype(BF16)
    ga_ref[0] = jax.nn.sigmoid(g_attn).astype(BF16)
    gs_ref[0] = jax.nn.sigmoid(g_ssm).astype(BF16)

    cos_sin = _dot_tn(_bf16_parts(rope_ref[0]), place_ref[...])
    cos = cos_sin[:, :HEAD_PAD] + base_ref[...]
    sin = cos_sin[:, HEAD_PAD:]

    cqn = _rms(c_q, qn_ref[...]).astype(BF16)
    q_t = _dot_nt(wq_ref[...], cqn)
    ckvn = _rms(c_kv, kvn_ref[...]).astype(BF16)
    k_nope = _dot(ckvn, wk_ref[...])
    k_rope = kr * cos + pltpu.roll(kr, HEAD_PAD - QK_ROPE_DIM, axis=1) * sin
    k_rope = k_rope + one_ref[...]
    half = rope_ref.shape[1] // 2
    cos_r, sin_r = rope_ref[0, :half, :], rope_ref[0, half:, :]
    tokens = cos_r.shape[1]
    cos_t = jnp.concatenate([jnp.ones((QK_NOPE_DIM, tokens), F32), cos_r, cos_r,
                             jnp.zeros((QK_ROPE_DIM, tokens), F32)], axis=0)
    sin_t = jnp.concatenate([jnp.zeros((QK_NOPE_DIM, tokens), F32), sin_r, sin_r,
                             jnp.zeros((QK_ROPE_DIM, tokens), F32)], axis=0)
    for h in range(MLA_HEADS):
        sl = slice(h * HEAD_PAD, (h + 1) * HEAD_PAD)
        blk = q_t[sl]
        partner = jnp.concatenate([blk[QK_ROPE_DIM:], blk[:QK_ROPE_DIM]], axis=0)
        q_ref[0, h] = ((blk * cos_t + partner * sin_t) * scale).astype(BF16)
        k_ref[0, h] = (k_nope[:, sl] + k_rope).astype(BF16)
    vt = _dot_nt(wvt_ref[...], ckvn).astype(BF16)
    ones = jnp.ones((V_ONES_ROWS, vt.shape[1]), BF16)
    vt_ref[0, 0] = jnp.concatenate(
        [piece for h in range(MLA_HEADS) for piece in (vt[h * V_HEAD_DIM:(h + 1) * V_HEAD_DIM], ones)], axis=0)


def _place_cols(w, starts, width, total):
    out = jnp.zeros((w.shape[0], total), w.dtype)
    for i, s in enumerate(starts):
        out = lax.dynamic_update_slice(out, w[:, i * width:(i + 1) * width], (0, s))
    return out


def _rot_half(w):
    half = w.shape[1] // 2
    return jnp.concatenate([-w[:, half:], w[:, :half]], axis=1)


def _inproj(h1, rope_t, mix_norm, w_in, q_norm, w_uq, kv_norm, w_ukv, ssm_w):
    b, s, d = h1.shape
    q_rank, kv_rank = q_norm.shape[0], kv_norm.shape[0]
    hq = QK_NOPE_DIM + QK_ROPE_DIM
    hkv = QK_NOPE_DIM + V_HEAD_DIM
    scale = float(hq) ** -0.5 * math.log2(math.e)
    o_kr = q_rank + kv_rank
    o_u = o_kr + QK_ROPE_DIM
    w_kr = w_in[:, o_kr:o_u]
    assert hq + QK_ROPE_DIM == HEAD_PAD
    kr_p = jnp.concatenate([jnp.zeros((d, QK_NOPE_DIM), w_in.dtype), w_kr, _rot_half(w_kr)], axis=1)
    win = jnp.concatenate([w_in[:, :o_kr], kr_p, w_in[:, o_u:]], axis=1).astype(BF16)
    wq = jnp.concatenate(
        [piece for h in range(MLA_HEADS)
         for piece in (w_uq[:, h * hq:(h + 1) * hq], _rot_half(w_uq[:, h * hq + QK_NOPE_DIM:(h + 1) * hq]))],
        axis=1).T.astype(BF16)
    wk = jnp.concatenate([jnp.pad(w_ukv[:, h * hkv:h * hkv + QK_NOPE_DIM], ((0, 0), (0, HEAD_PAD - QK_NOPE_DIM)))
                          for h in range(MLA_HEADS)], axis=1).astype(BF16)
    wvt = jnp.concatenate([w_ukv[:, h * hkv + QK_NOPE_DIM:(h + 1) * hkv] for h in range(MLA_HEADS)],
                          axis=1).T.astype(BF16)
    half = QK_ROPE_DIM // 2
    eye = jnp.eye(half, dtype=F32)
    place1 = _place_cols(jnp.concatenate([eye, eye], axis=1), [QK_NOPE_DIM], QK_ROPE_DIM, HEAD_PAD)
    zero = jnp.zeros_like(place1)
    place = jnp.concatenate([jnp.concatenate([place1, zero], axis=1),
                             jnp.concatenate([zero, place1], axis=1)], axis=0)
    place = jnp.tile(place, (3, 1)).astype(BF16)
    base = (jnp.arange(HEAD_PAD) < QK_NOPE_DIM).astype(F32).reshape(1, HEAD_PAD)

    tm = min(TOKEN_TILE, s)
    nt = s // tm
    hv = MLA_HEADS * (V_HEAD_DIM + V_ONES_ROWS)
    const = lambda shape: pl.BlockSpec(shape, lambda i, j: (0,) * len(shape))
    tok = lambda w: pl.BlockSpec((1, tm, w), lambda i, j: (i, j, 0))
    heads = pl.BlockSpec((1, MLA_HEADS, tm, HEAD_PAD), lambda i, j: (i, 0, j, 0))
    heads_t = pl.BlockSpec((1, MLA_HEADS, HEAD_PAD, tm), lambda i, j: (i, 0, 0, j))
    one = (jnp.arange(HEAD_PAD) == AUG_LANE).astype(F32).reshape(1, HEAD_PAD)
    rope = pl.BlockSpec((1, 2 * half, tm), lambda i, j: (i, 0, j))
    dims = (q_rank, kv_rank, ssm_w, d, scale)
    return pl.pallas_call(
        functools.partial(_inproj_body, dims=dims),
        grid=(b, nt),
        in_specs=[tok(d), const((1, d)), const(win.shape), const((1, q_rank)), const((1, kv_rank)),
                  const(wq.shape), const(wk.shape), const(wvt.shape),
                  rope, const(place.shape), const(base.shape), const(one.shape)],
        out_specs=[heads_t, heads, pl.BlockSpec((1, 1, hv, tm), lambda i, j: (i, j, 0, 0)),
                   tok(ssm_w), tok(d), tok(d)],
        out_shape=[jax.ShapeDtypeStruct((b, MLA_HEADS, HEAD_PAD, s), BF16),
                   jax.ShapeDtypeStruct((b, MLA_HEADS, s, HEAD_PAD), BF16),
                   jax.ShapeDtypeStruct((b, nt, hv, tm), BF16),
                   jax.ShapeDtypeStruct((b, s, ssm_w), BF16),
                   jax.ShapeDtypeStruct((b, s, d), BF16),
                   jax.ShapeDtypeStruct((b, s, d), BF16)],
        compiler_params=_cparams(("parallel", "parallel")),
        name="inproj",
    )(h1, mix_norm.reshape(1, d).astype(F32), win, q_norm.reshape(1, q_rank).astype(F32),
      kv_norm.reshape(1, kv_rank).astype(F32), wq, wk, wvt, rope_t, place, base, one)


def _attn_body(q_ref, k_ref, vt_ref, o_ref, s_ref, acc_ref, *, n_kt, tk):
    qt = q_ref[0, 0]
    tq = qt.shape[1]
    pack = 2 * SUBLANES
    first = lax.broadcasted_iota(jnp.int32, (pack, tq), 0) == 0

    def with_reference(m):
        row = jnp.where(first, jnp.broadcast_to(-m, (pack, tq)), 0.0).astype(BF16)
        return jnp.concatenate([qt[:AUG_LANE], row, qt[AUG_LANE + pack:]], axis=0)

    def rounded(m):
        return m.astype(BF16).astype(F32)

    def finish():
        acc = acc_ref[...]
        o_ref[0] = (acc[:V_HEAD_DIM] / acc[V_HEAD_DIM:V_HEAD_DIM + 1]).astype(BF16)

    s = _dot(k_ref[0, 0, 0:tk, :], qt)
    m = rounded(jnp.max(s, axis=0, keepdims=True))
    p_prev = jnp.exp2(s - m).astype(BF16)
    beta = jnp.ones_like(m)
    acc_ref[...] = jnp.zeros_like(acc_ref)
    gap = jnp.zeros_like(m)
    for j in range(1, n_kt):
        q_aug = with_reference(m)
        k_tile = k_ref[0, 0, j * tk:(j + 1) * tk, :]
        parts = []
        for i in range(tq // ATTN_SPLIT):
            cols = slice(i * ATTN_SPLIT, (i + 1) * ATTN_SPLIT)
            parts.append(jnp.exp2(_dot(k_tile, q_aug[:, cols])).astype(BF16))
            acc_ref[:, cols] = (acc_ref[:, cols] + _dot(vt_ref[0, j - 1], p_prev[:, cols])) * beta[:, cols]
        p = jnp.concatenate(parts, axis=1)
        g = jnp.log2(jnp.max(p, axis=0, keepdims=True).astype(F32))
        gap = jnp.maximum(gap, g)
        m_new = rounded(m + jnp.maximum(g, 0.0))
        beta = jnp.exp2(m - m_new)
        m = m_new
        p_prev = p
    acc_ref[...] = (acc_ref[...] + _dot(vt_ref[0, n_kt - 1], p_prev)) * beta
    finish()

    @pl.when(jnp.logical_not(jnp.max(gap) <= ATTN_MAX_GAP))
    def _():
        def tile(j, m):
            start = pl.multiple_of(j * tk, tk)
            s_ref[...] = _dot(k_ref[0, 0, pl.ds(start, tk), :], qt)
            s = s_ref[...]
            m_new = jnp.maximum(m, jnp.max(s, axis=0, keepdims=True))
            p = jnp.exp2(s - m_new).astype(BF16)
            acc_ref[...] = jnp.exp2(m - m_new) * acc_ref[...] + _dot(vt_ref[0, j], p)
            return m_new

        acc_ref[...] = jnp.zeros_like(acc_ref)
        lax.fori_loop(0, n_kt, tile, jnp.full((1, tq), -jnp.inf, F32))
        finish()


def _attention(q_t, k, vt):
    b, h, s, _ = k.shape
    n_kt, tk = vt.shape[1], vt.shape[3]
    vrows = vt.shape[2] // h
    tq = min(ATTN_TQ, s)
    assert tq % ATTN_SPLIT == 0
    return pl.pallas_call(
        functools.partial(_attn_body, n_kt=n_kt, tk=tk),
        grid=(b, h, s // tq),
        in_specs=[pl.BlockSpec((1, 1, HEAD_PAD, tq), lambda i, j, t: (i, j, 0, t)),
                  pl.BlockSpec((1, 1, s, HEAD_PAD), lambda i, j, t: (i, j, 0, 0)),
                  pl.BlockSpec((1, n_kt, vrows, tk), lambda i, j, t: (i, 0, j, 0))],
        out_specs=pl.BlockSpec((1, V_HEAD_DIM, tq), lambda i, j, t: (i, j, t)),
        out_shape=jax.ShapeDtypeStruct((b, h * V_HEAD_DIM, s), BF16),
        scratch_shapes=[pltpu.VMEM((tk, tq), F32), pltpu.VMEM((vrows, tq), F32)],
        compiler_params=_cparams(("parallel", "parallel", "arbitrary")),
        name="attention",
    )(q_t, k, vt)


def _discretise(lr, li, ldt):
    dt = jnp.exp(ldt)
    mag = jnp.exp(lr * dt)
    lb_re = mag * jnp.cos(li * dt)
    lb_im = mag * jnp.sin(li * dt)
    den = lr * lr + li * li
    nr = lb_re - 1.0
    return (nr * lr + lb_im * li) / den, (lb_im * lr - nr * li) / den, lr * dt, li * dt


def _ssm_prep_body(rowp_ref, colp_ref, bt_re_ref, bt_im_ref, bc_re_ref, bc_im_ref, cr_re_ref, cr_im_ref, dt_ref,
                   tt_ref, pp_ref, qq_ref, dec_ref, *, n_state):
    L = SSM_CHUNK
    rows = tt_ref.shape[1]
    width = rows // L
    lanes2 = 2 * n_state
    parity = pl.program_id(0) % 2
    lane = lax.broadcasted_iota(jnp.int32, (1, lanes2), 1)
    keep_lane = (lane >= parity * n_state) & (lane < (parity + 1) * n_state)
    first_lane = lane < n_state
    n_pow = (L + 1 + SUBLANES - 1) // SUBLANES * SUBLANES
    kpow = lax.broadcasted_iota(jnp.int32, (n_pow, 1), 0).astype(F32)
    rep = (lax.broadcasted_iota(jnp.int32, (rows + width, n_pow), 0) // width
           == lax.broadcasted_iota(jnp.int32, (rows + width, n_pow), 1)).astype(F32)
    sl = lax.broadcasted_iota(jnp.int32, (1, rows), 1) // width
    blk = lambda a, k: a[k * width:(k + 1) * width]

    k_rows = []
    for d in range(2):
        k_re, k_im, ar, ai = _discretise(rowp_ref[0, d, 0:1, :], rowp_ref[0, d, 1:2, :], rowp_ref[0, d, 2:3, :])
        magk = jnp.exp(kpow * ar)
        ak_re = _dot_exact(rep, magk * jnp.cos(kpow * ai))
        ak_im = _dot_exact(rep, magk * jnp.sin(kpow * ai))
        bt_re, bt_im = bt_re_ref[0, d], bt_im_ref[0, d]
        bb_re = k_re * bt_re - k_im * bt_im
        bb_im = k_re * bt_im + k_im * bt_re
        ba_re = bb_re * ak_re[:rows] - bb_im * ak_im[:rows]
        ba_im = bb_re * ak_im[:rows] + bb_im * ak_re[:rows]
        blocks = range(L - 1, -1, -1) if d == 0 else range(L)
        p_re = jnp.concatenate([blk(ba_re, k) for k in blocks], axis=0)
        p_im = jnp.concatenate([blk(ba_im, k) for k in blocks], axis=0)
        pp_ref[0, :, (2 * d) * lanes2:(2 * d + 1) * lanes2] = jnp.where(keep_lane, p_re, 0.0).astype(BF16)
        pp_ref[0, :, (2 * d + 1) * lanes2:(2 * d + 2) * lanes2] = jnp.where(keep_lane, p_im, 0.0).astype(BF16)
        cr_re, cr_im = cr_re_ref[0, d], cr_im_ref[0, d]
        ca_re = cr_re * ak_re - cr_im * ak_im
        ca_im = cr_re * ak_im + cr_im * ak_re
        blocks = range(1, L + 1) if d == 0 else range(L, 0, -1)
        q_re = jnp.concatenate([blk(ca_re, k) for k in blocks], axis=0)
        q_im = jnp.concatenate([blk(ca_im, k) for k in blocks], axis=0)
        qq_ref[0, :, (2 * d) * lanes2:(2 * d + 1) * lanes2] = jnp.where(keep_lane, q_re, 0.0).astype(BF16)
        qq_ref[0, :, (2 * d + 1) * lanes2:(2 * d + 2) * lanes2] = jnp.where(keep_lane, -q_im, 0.0).astype(BF16)
        kc_re, kc_im, _, _ = _discretise(colp_ref[0, d, :, 0:1], colp_ref[0, d, :, 1:2], colp_ref[0, d, :, 2:3])
        bc_re, bc_im = bc_re_ref[0, d], bc_im_ref[0, d]
        bbc_re = kc_re * bc_re - kc_im * bc_im
        bbc_im = kc_re * bc_im + kc_im * bc_re
        k_rows.append(_dot_exact(jnp.where(first_lane, ca_re[:rows], 0.0), bbc_re)
                      - _dot_exact(jnp.where(first_lane, ca_im[:rows], 0.0), bbc_im))
        dec_ref[0, 2 * d:2 * d + 1, :] = jnp.where(keep_lane, ak_re[rows:rows + 1], 0.0)
        dec_ref[0, 2 * d + 1:2 * d + 2, :] = jnp.where(keep_lane, ak_im[rows:rows + 1], 0.0)
    dec_ref[0, 4:8, :] = jnp.zeros((4, lanes2), F32)

    kf, kb = k_rows
    diag = blk(kf, 0) + blk(kb, 0) + dt_ref[0]
    for t in range(L):
        acc = jnp.where(sl == t, diag, 0.0)
        for k in range(1, t + 1):
            acc = jnp.where(sl == t - k, blk(kf, k), acc)
        for k in range(1, L - t):
            acc = jnp.where(sl == t + k, blk(kb, k), acc)
        tt_ref[0, t * width:(t + 1) * width, :] = acc.astype(BF16)


def _ssm_prep(lam_re, lam_im, log_dt, b_re, b_im, c_re, c_im, d_skip):
    _, g, p, w = b_re.shape
    L = SSM_CHUNK
    rows = L * w
    dup = lambda a, axis: jnp.concatenate([a, a], axis=axis)
    ldt = jnp.broadcast_to(log_dt[:, :, None], lam_re.shape)
    params = [dup(lam_re, -1), dup(lam_im, -1), dup(ldt, -1)]
    rowp = jnp.pad(jnp.stack(params, axis=2), ((0, 0), (0, 0), (0, SUBLANES - 3), (0, 0)))
    colp = jnp.pad(jnp.stack(params, axis=3), ((0, 0), (0, 0), (0, 0), (0, LANES - 3)))
    per_group = lambda a: jnp.swapaxes(a, 0, 1).astype(F32)
    bt = lambda b: jnp.tile(dup(jnp.swapaxes(b, 2, 3), -1), (1, 1, L, 1))
    bc = lambda b: jnp.tile(dup(b, 2), (1, 1, 1, L))
    cr = lambda c: jnp.tile(dup(c, -1), (1, 1, L + 1, 1))
    dmat = jnp.where(jnp.eye(w, dtype=bool)[None], d_skip.reshape(g, 1, w), 0.0)
    dtile = jnp.tile(dmat, (1, 1, L)).astype(F32)
    grp = lambda shape: pl.BlockSpec((1,) + shape, lambda i: (i,) + (0,) * len(shape))
    return pl.pallas_call(
        functools.partial(_ssm_prep_body, n_state=p),
        grid=(g,),
        in_specs=[grp((2, SUBLANES, 2 * p)), grp((2, 2 * p, LANES)), grp((2, rows, 2 * p)), grp((2, rows, 2 * p)),
                  grp((2, 2 * p, rows)), grp((2, 2 * p, rows)), grp((2, rows + w, 2 * p)), grp((2, rows + w, 2 * p)),
                  grp((w, rows))],
        out_specs=[grp((rows, rows)), grp((rows, 8 * p)), grp((rows, 8 * p)), grp((SUBLANES, 2 * p))],
        out_shape=[jax.ShapeDtypeStruct((g, rows, rows), BF16),
                   jax.ShapeDtypeStruct((g, rows, 8 * p), BF16),
                   jax.ShapeDtypeStruct((g, rows, 8 * p), BF16),
                   jax.ShapeDtypeStruct((g, SUBLANES, 2 * p), F32)],
        compiler_params=_cparams(("parallel",)),
        name="ssm_prep",
    )(per_group(rowp), per_group(colp), per_group(bt(b_re)), per_group(bt(b_im)), per_group(bc(b_re)),
      per_group(bc(b_im)), per_group(cr(c_re)), per_group(cr(c_im)), dtile)


def _utile_body(u_ref, eye_ref, o_ref, scr_ref):
    scr_ref[...] = u_ref[0].astype(F32)
    half = SSM_CHUNK // 2
    for sh in range(2):
        rows = jnp.concatenate([scr_ref[:, sh * half + s, :] for s in range(half)], axis=0).astype(BF16)
        o_ref[0, sh] = _dot_nt(eye_ref[...], rows).astype(BF16)


def _utile(u):
    b, s, w = u.shape
    L = SSM_CHUNK
    n_cb = s // (L * LANES)
    u4 = u.reshape(b, s // L, L, w)
    eye = jnp.eye(w, dtype=BF16)
    return pl.pallas_call(
        _utile_body,
        grid=(b, n_cb),
        in_specs=[pl.BlockSpec((1, LANES, L, w), lambda i, j: (i, j, 0, 0)),
                  pl.BlockSpec((w, w), lambda i, j: (0, 0))],
        out_specs=pl.BlockSpec((1, 2, w, LANES * L // 2), lambda i, j: (i, j, 0, 0)),
        out_shape=jax.ShapeDtypeStruct((b, 2 * n_cb, w, LANES * L // 2), BF16),
        scratch_shapes=[pltpu.VMEM((LANES, L, w), F32)],
        compiler_params=_cparams(("parallel", "parallel")),
        name="ssm_utile",
    )(u4, eye)


def _ssm_body(u_ref, tt_ref, pp_ref, qq_ref, dec_ref, y_ref, st_ref, *, n_chunks):
    bsz = u_ref.shape[0]
    n_cb = u_ref.shape[1] // 2
    width = u_ref.shape[2] // 2
    L = SSM_CHUNK
    half = L // 2
    w = LANES
    assert bsz == SUBLANES

    def tile_of(cb, step):
        return 2 * cb + step // half, slice((step % half) * w, (step % half + 1) * w)

    def ucol(b, g):
        pieces = []
        for s in range(L):
            row = []
            for cb in range(n_cb):
                tile, lanes = tile_of(cb, s)
                row.append(u_ref[b, tile, g * width:(g + 1) * width, lanes])
            pieces.append(jnp.concatenate(row, axis=1))
        return jnp.concatenate(pieces, axis=0)

    for b in range(bsz):
        e = _dot_tn(ucol(b, 0), pp_ref[0]) + _dot_tn(ucol(b, 1), pp_ref[1])
        for i in range(4):
            st_ref[i, pl.ds(b, n_chunks, stride=bsz), :] = e[:, i * w:(i + 1) * w]

    dec = dec_ref[0] + dec_ref[1]
    a_fr, a_fi, a_br, a_bi = (jnp.broadcast_to(dec[i:i + 1, :], (bsz, w)) for i in range(4))

    def step(c, carry):
        hfr, hfi, hbr, hbi = carry
        rf = pl.multiple_of(c * bsz, bsz)
        rb = pl.multiple_of((n_chunks - 1 - c) * bsz, bsz)
        efr = st_ref[0, pl.ds(rf, bsz), :]
        efi = st_ref[1, pl.ds(rf, bsz), :]
        ebr = st_ref[2, pl.ds(rb, bsz), :]
        ebi = st_ref[3, pl.ds(rb, bsz), :]
        st_ref[0, pl.ds(rf, bsz), :] = hfr
        st_ref[1, pl.ds(rf, bsz), :] = hfi
        st_ref[2, pl.ds(rb, bsz), :] = hbr
        st_ref[3, pl.ds(rb, bsz), :] = hbi
        return (a_fr * hfr - a_fi * hfi + efr, a_fi * hfr + a_fr * hfi + efi,
                a_br * hbr - a_bi * hbi + ebr, a_bi * hbr + a_br * hbi + ebi)

    zero = jnp.zeros((bsz, w), F32)
    lax.fori_loop(0, n_chunks, step, (zero, zero, zero, zero))

    for b in range(bsz):
        h = jnp.concatenate([st_ref[i, pl.ds(b, n_chunks, stride=bsz), :] for i in range(4)],
                            axis=1).astype(BF16)
        for g in range(2):
            y = (_dot(tt_ref[g], ucol(b, g)) + _dot_nt(qq_ref[g], h)).astype(BF16)
            for t in range(L):
                for cb in range(n_cb):
                    tile, lanes = tile_of(cb, t)
                    y_ref[b, tile, g * width:(g + 1) * width, lanes] = y[t * width:(t + 1) * width, cb * w:(cb + 1) * w]


def _ssm(u_t, tt, pp, qq, dec):
    b, n_tiles, ssm_w, tl = u_t.shape
    g, k, _ = tt.shape
    n_state8 = pp.shape[2]
    width = k // SSM_CHUNK
    n_chunks = n_tiles // 2 * LANES
    pair = lambda shape: pl.BlockSpec((2,) + shape, lambda i: (i,) + (0,) * len(shape))
    seq = pl.BlockSpec((b, n_tiles, 2 * width, tl), lambda i: (0, 0, i, 0))
    return pl.pallas_call(
        functools.partial(_ssm_body, n_chunks=n_chunks),
        grid=(g // 2,),
        in_specs=[seq, pair((k, k)), pair((k, n_state8)), pair((k, n_state8)), pair((SUBLANES, dec.shape[2]))],
        out_specs=seq,
        out_shape=jax.ShapeDtypeStruct(u_t.shape, BF16),
        scratch_shapes=[pltpu.VMEM((n_state8 // LANES, n_chunks * b, LANES), F32)],
        compiler_params=_cparams(("parallel",)),
        name="ssm",
    )(u_t, tt, pp, qq, dec)


def _ssm_post_body(y_ref, wglu_ref, bglu_ref, wos_ref, o_ref):
    ssm_w = y_ref.shape[2]
    y = y_ref[0, 0].astype(F32)
    glu = _dot_tn(jax.nn.gelu(y).astype(BF16), wglu_ref[...]) + bglu_ref[...]
    ssm_out = glu[:, :ssm_w] * jax.nn.sigmoid(glu[:, ssm_w:])
    out = _dot(ssm_out.astype(BF16), wos_ref[...])
    for s in range(SSM_CHUNK // 2):
        o_ref[0, :, s, :] = out[s * LANES:(s + 1) * LANES]


def _ssm_post(y_t, w_glu, b_glu, w_o_ssm):
    b, n_tiles, ssm_w, tl = y_t.shape
    d = w_o_ssm.shape[1]
    L = SSM_CHUNK
    s = n_tiles * tl
    const = lambda shape: pl.BlockSpec(shape, lambda i, j: (0,) * len(shape))
    out = pl.pallas_call(
        _ssm_post_body,
        grid=(b, n_tiles),
        in_specs=[pl.BlockSpec((1, 1, ssm_w, tl), lambda i, j: (i, j, 0, 0)),
                  const(w_glu.shape), const((1, w_glu.shape[1])), const(w_o_ssm.shape)],
        out_specs=pl.BlockSpec((1, LANES, L // 2, d), lambda i, j: (i, j // 2, j % 2, 0)),
        out_shape=jax.ShapeDtypeStruct((b, s // L, L, d), F32),
        compiler_params=_cparams(("parallel", "parallel")),
        name="ssm_post",
    )(y_t, w_glu.astype(BF16), b_glu.reshape(1, -1).astype(F32), w_o_ssm.astype(BF16))
    return out.reshape(b, s, d)


def _merge_ffn_body(h_ref, at_ref, ssm_ref, ga_ref, gs_ref, woa_ref, wout_ref, g_ref, wg_ref, wu_ref, wd_ref,
                    *rest):
    o_ref = rest[-1]
    attn = _dot_tn(at_ref[0], woa_ref[...])
    merged = ga_ref[0].astype(F32) * attn + gs_ref[0].astype(F32) * ssm_ref[0]
    h = h_ref[0] + _dot(merged.astype(BF16), wout_ref[...])
    h = _ffn_apply(h, g_ref, wg_ref, wu_ref, wd_ref)
    if len(rest) == 2:
        h = _rms(h, rest[0][...])
    o_ref[0] = h


def _merge_ffn(h1, attn_t, ssm, ga, gs, w_o_attn, w_out, norm, w_gate, w_up, w_down, final_norm=None):
    b, s, d = h1.shape
    hv = attn_t.shape[1]
    tm = min(TOKEN_TILE, s)
    tok = lambda w: pl.BlockSpec((1, tm, w), lambda i, j: (i, j, 0))
    ffn_args, ffn_specs = _ffn_operands(norm, w_gate, w_up, w_down)
    args = [h1, attn_t, ssm, ga, gs, w_o_attn.astype(BF16), w_out.astype(BF16)] + ffn_args
    in_specs = [tok(d), pl.BlockSpec((1, hv, tm), lambda i, j: (i, 0, j)), tok(d), tok(d), tok(d),
                _resident(w_o_attn.shape), _resident(w_out.shape)] + ffn_specs
    if final_norm is not None:
        args.append(final_norm.reshape(1, d).astype(F32))
        in_specs.append(_resident((1, d)))
    return pl.pallas_call(
        _merge_ffn_body,
        grid=(b, s // tm),
        in_specs=in_specs,
        out_specs=tok(d),
        out_shape=jax.ShapeDtypeStruct((b, s, d), F32),
        compiler_params=_cparams(("parallel", "parallel")),
        name="merge_ffn",
    )(*args)


def kernel(x, positions, ffn1_norm, ffn1_w_gate, ffn1_w_up, ffn1_w_down, mix_norm, w_in, q_norm, w_uq, kv_norm, w_ukv, w_o_attn, ssm_lambda_re, ssm_lambda_im, ssm_log_dt, ssm_b_re, ssm_b_im, ssm_c_re, ssm_c_im, ssm_d, w_glu, b_glu, w_o_ssm, w_out, ffn2_norm, ffn2_w_gate, ffn2_w_up, ffn2_w_down, final_norm):
    b, s, d = x.shape
    depth = ffn1_norm.shape[0]
    _, _, n_grp, _, grp_w = ssm_b_re.shape
    ssm_w = n_grp * grp_w
    L = SSM_CHUNK
    assert depth >= 1, "the final norm is fused into the last layer's second FFN"
    assert b == SUBLANES, "the S5 chunk recurrence keeps the batch on the 8 sublanes"
    assert s % max(TOKEN_TILE, ATTN_TQ, L * LANES) == 0 and L * grp_w == MXU_DIM and n_grp % 2 == 0

    rope_t = _rope_tables(positions)
    h = x
    for l in range(depth):
        last = l == depth - 1
        h = _ffn(h.reshape(b * s, d), ffn1_norm[l], ffn1_w_gate[l], ffn1_w_up[l], ffn1_w_down[l]).reshape(b, s, d)
        q, k, vt, u, ga, gs = _inproj(h, rope_t, mix_norm[l], w_in[l], q_norm[l], w_uq[l],
                                      kv_norm[l], w_ukv[l], ssm_w)
        attn_t = _attention(q, k, vt)

        tt, pp, qq, dec = _ssm_prep(ssm_lambda_re[l], ssm_lambda_im[l], ssm_log_dt[l], ssm_b_re[l], ssm_b_im[l],
                                    ssm_c_re[l], ssm_c_im[l], ssm_d[l])
        y_t = _ssm(_utile(u), tt, pp, qq, dec)
        ssm = _ssm_post(y_t, w_glu[l], b_glu[l], w_o_ssm[l])

        h = _merge_ffn(h, attn_t, ssm, ga, gs, w_o_attn[l], w_out[l], ffn2_norm[l], ffn2_w_gate[l],
                       ffn2_w_up[l], ffn2_w_down[l], final_norm if last else None)
    return h
```

```python
import functools
import math

import jax
import jax.numpy as jnp
from jax import lax
from jax.experimental import pallas as pl
from jax.experimental.pallas import tpu as pltpu

F32 = jnp.float32
BF16 = jnp.bfloat16

MLA_HEADS = 8
QK_NOPE_DIM = 64
QK_ROPE_DIM = 32
V_HEAD_DIM = 64
ROPE_THETA = 10000.0
FFN_RES = 0.5
EPS = 1e-6

LANES = 128
SUBLANES = 8
MXU_DIM = 256
VMEM_LIMIT_BYTES = 56 * 1024 * 1024

HEAD_PAD = LANES
AUG_LANE = 96
V_ONES_ROWS = 16
SSM_CHUNK = 16

TOKEN_TILE = 512
ATTN_TQ = 2048
ATTN_SPLIT = 512
FF_CHUNK = 256
ATTN_MAX_GAP = 64.0

def _cparams(sem, flags=None):
    return pltpu.CompilerParams(dimension_semantics=sem, vmem_limit_bytes=VMEM_LIMIT_BYTES, flags=flags)


def _rms(x, g):
    return x * lax.rsqrt(jnp.mean(x * x, axis=-1, keepdims=True) + EPS) * g


def _dot(a, b):
    return jnp.dot(a, b, preferred_element_type=F32)


def _dot_nt(a, b):
    return lax.dot_general(a, b, (((1,), (1,)), ((), ())), preferred_element_type=F32)


def _dot_tn(a, b):
    return lax.dot_general(a, b, (((0,), (0,)), ((), ())), preferred_element_type=F32)


def _dot_exact(a, b):
    return jnp.dot(a, b, preferred_element_type=F32, precision=lax.Precision.HIGHEST)


def _bf16_parts(a):
    hi = a.astype(BF16)
    rest = a - hi.astype(F32)
    mid = rest.astype(BF16)
    lo = (rest - mid.astype(F32)).astype(BF16)
    return jnp.concatenate([hi, mid, lo], axis=0)


def _rope_body(pos_ref, invf_ref, o_ref):
    half = invf_ref.shape[0]
    ang = invf_ref[...] * pos_ref[0]
    o_ref[0, :half, :] = jnp.cos(ang)
    o_ref[0, half:, :] = jnp.sin(ang)


def _rope_tables(positions):
    b, s = positions.shape
    half = QK_ROPE_DIM // 2
    inv_freq = ROPE_THETA ** (-jnp.arange(0, QK_ROPE_DIM, 2, dtype=F32) / QK_ROPE_DIM)
    return pl.pallas_call(
        _rope_body,
        grid=(b,),
        in_specs=[pl.BlockSpec((1, 1, s), lambda i: (i, 0, 0)), pl.BlockSpec((half, 1), lambda i: (0, 0))],
        out_specs=pl.BlockSpec((1, 2 * half, s), lambda i: (i, 0, 0)),
        out_shape=jax.ShapeDtypeStruct((b, 2 * half, s), F32),
        compiler_params=_cparams(("parallel",)),
        name="rope_tables",
    )(positions.astype(F32).reshape(b, 1, s), inv_freq.reshape(half, 1))


def _ffn_apply(x, g_ref, wg_ref, wu_ref, wd_ref):
    n = _rms(x, g_ref[...]).astype(BF16)
    acc = jnp.zeros_like(x)
    for j in range(wg_ref.shape[1] // FF_CHUNK):
        cols = slice(j * FF_CHUNK, (j + 1) * FF_CHUNK)
        gate = _dot(n, wg_ref[:, cols])
        up = _dot(n, wu_ref[:, cols])
        act = (gate * jax.nn.sigmoid(gate) * up).astype(BF16)
        acc = acc + _dot(act, wd_ref[cols, :])
    return x + FFN_RES * acc


def _resident(shape):
    return pl.BlockSpec(shape, lambda *_: (0,) * len(shape), pipeline_mode=pl.Buffered(1))


def _ffn_operands(norm, w_gate, w_up, w_down):
    d, d_ff = w_gate.shape
    assert d_ff % FF_CHUNK == 0
    args = [norm.reshape(1, d).astype(F32), w_gate.astype(BF16), w_up.astype(BF16), w_down.astype(BF16)]
    return args, [_resident(a.shape) for a in args]


def _inproj_body(x_ref, fg_ref, fwg_ref, fwu_ref, fwd_ref,
                 g_ref, win_ref, qn_ref, kvn_ref, wq_ref, wk_ref, wvt_ref,
                 rope_ref, place_ref, base_ref, one_ref,
                 h_ref, q_ref, k_ref, vt_ref, u_ref, ga_ref, gs_ref, *, dims):
    q_rank, kv_rank, ssm_w, d_model, scale = dims
    h = _ffn_apply(x_ref[0], fg_ref, fwg_ref, fwu_ref, fwd_ref)
    h_ref[0] = h
    n = _rms(h, g_ref[...]).astype(BF16)
    z = _dot(n, win_ref[...])
    o = 0
    c_q = z[:, o:o + q_rank]; o += q_rank
    c_kv = z[:, o:o + kv_rank]; o += kv_rank
    kr = z[:, o:o + HEAD_PAD]; o += HEAD_PAD
    u = z[:, o:o + ssm_w]; o += ssm_w
    g_attn = z[:, o:o + d_model]; o += d_model
    g_ssm = z[:, o:o + d_model]

    u_ref[0] = u.astype(BF16)
    ga_ref[0] = jax.nn.sigmoid(g_attn).astype(BF16)
    gs_ref[0] = jax.nn.sigmoid(g_ssm).astype(BF16)

    cos_sin = _dot_tn(_bf16_parts(rope_ref[0]), place_ref[...])
    cos = cos_sin[:, :HEAD_PAD] + base_ref[...]
    sin = cos_sin[:, HEAD_PAD:]

    cqn = _rms(c_q, qn_ref[...]).astype(BF16)
    q_t = _dot_nt(wq_ref[...], cqn)
    ckvn = _rms(c_kv, kvn_ref[...]).astype(BF16)
    k_nope = _dot(ckvn, wk_ref[...])
    k_rope = kr * cos + pltpu.roll(kr, HEAD_PAD - QK_ROPE_DIM, axis=1) * sin
    k_rope = k_rope + one_ref[...]
    half = rope_ref.shape[1] // 2
    cos_r, sin_r = rope_ref[0, :half, :], rope_ref[0, half:, :]
    tokens = cos_r.shape[1]
    cos_t = jnp.concatenate([jnp.ones((QK_NOPE_DIM, tokens), F32), cos_r, cos_r,
                             jnp.zeros((QK_ROPE_DIM, tokens), F32)], axis=0)
    sin_t = jnp.concatenate([jnp.zeros((QK_NOPE_DIM, tokens), F32), sin_r, sin_r,
                             jnp.zeros((QK_ROPE_DIM, tokens), F32)], axis=0)
    for h in range(MLA_HEADS):
        sl = slice(h * HEAD_PAD, (h + 1) * HEAD_PAD)
        blk = q_t[sl]
        partner = jnp.concatenate([blk[QK_ROPE_DIM:], blk[:QK_ROPE_DIM]], axis=0)
        q_ref[0, h] = ((blk * cos_t + partner * sin_t) * scale).astype(BF16)
        k_ref[0, h] = (k_nope[:, sl] + k_rope).astype(BF16)
    vt = _dot_nt(wvt_ref[...], ckvn).astype(BF16)
    ones = jnp.ones((V_ONES_ROWS, vt.shape[1]), BF16)
    vt_ref[0, 0] = jnp.concatenate(
        [piece for h in range(MLA_HEADS) for piece in (vt[h * V_HEAD_DIM:(h + 1) * V_HEAD_DIM], ones)], axis=0)


def _place_cols(w, starts, width, total):
    out = jnp.zeros((w.shape[0], total), w.dtype)
    for i, s in enumerate(starts):
        out = lax.dynamic_update_slice(out, w[:, i * width:(i + 1) * width], (0, s))
    return out


def _rot_half(w):
    half = w.shape[1] // 2
    return jnp.concatenate([-w[:, half:], w[:, :half]], axis=1)


def _ffn_inproj(x, ffn_params, rope_t, mix_norm, w_in, q_norm, w_uq, kv_norm, w_ukv, ssm_w):
    b, s, d = x.shape
    q_rank, kv_rank = q_norm.shape[0], kv_norm.shape[0]
    hq = QK_NOPE_DIM + QK_ROPE_DIM
    hkv = QK_NOPE_DIM + V_HEAD_DIM
    scale = float(hq) ** -0.5 * math.log2(math.e)
    o_kr = q_rank + kv_rank
    o_u = o_kr + QK_ROPE_DIM
    w_kr = w_in[:, o_kr:o_u]
    assert hq + QK_ROPE_DIM == HEAD_PAD
    kr_p = jnp.concatenate([jnp.zeros((d, QK_NOPE_DIM), w_in.dtype), w_kr, _rot_half(w_kr)], axis=1)
    win = jnp.concatenate([w_in[:, :o_kr], kr_p, w_in[:, o_u:]], axis=1).astype(BF16)
    wq = jnp.concatenate(
        [piece for h in range(MLA_HEADS)
         for piece in (w_uq[:, h * hq:(h + 1) * hq], _rot_half(w_uq[:, h * hq + QK_NOPE_DIM:(h + 1) * hq]))],
        axis=1).T.astype(BF16)
    wk = jnp.concatenate([jnp.pad(w_ukv[:, h * hkv:h * hkv + QK_NOPE_DIM], ((0, 0), (0, HEAD_PAD - QK_NOPE_DIM)))
                          for h in range(MLA_HEADS)], axis=1).astype(BF16)
    wvt = jnp.concatenate([w_ukv[:, h * hkv + QK_NOPE_DIM:(h + 1) * hkv] for h in range(MLA_HEADS)],
                          axis=1).T.astype(BF16)
    half = QK_ROPE_DIM // 2
    eye = jnp.eye(half, dtype=F32)
    place1 = _place_cols(jnp.concatenate([eye, eye], axis=1), [QK_NOPE_DIM], QK_ROPE_DIM, HEAD_PAD)
    zero = jnp.zeros_like(place1)
    place = jnp.concatenate([jnp.concatenate([place1, zero], axis=1),
                             jnp.concatenate([zero, place1], axis=1)], axis=0)
    place = jnp.tile(place, (3, 1)).astype(BF16)
    base = (jnp.arange(HEAD_PAD) < QK_NOPE_DIM).astype(F32).reshape(1, HEAD_PAD)

    tm = min(TOKEN_TILE, s)
    nt = s // tm
    hv = MLA_HEADS * (V_HEAD_DIM + V_ONES_ROWS)
    const = _resident
    ffn_args, ffn_specs = _ffn_operands(*ffn_params)
    tok = lambda w: pl.BlockSpec((1, tm, w), lambda i, j: (i, j, 0))
    heads =pl.BlockSpec((1, MLA_HEADS, tm, HEAD_PAD), lambda i, j: (i, 0, j, 0))
    heads_t = pl.BlockSpec((1, MLA_HEADS, HEAD_PAD, tm), lambda i, j: (i, 0, 0, j))
    one = (jnp.arange(HEAD_PAD) == AUG_LANE).astype(F32).reshape(1, HEAD_PAD)
    rope = pl.BlockSpec((1, 2 * half, tm), lambda i, j: (i, 0, j))
    dims = (q_rank, kv_rank, ssm_w, d, scale)
    return pl.pallas_call(
        functools.partial(_inproj_body, dims=dims),
        grid=(b, nt),
        in_specs=[tok(d)] + ffn_specs + [
                  const((1, d)), const(win.shape), const((1, q_rank)), const((1, kv_rank)),
                  const(wq.shape), const(wk.shape), const(wvt.shape),
                  rope, const(place.shape), const(base.shape), const(one.shape)],
        out_specs=[tok(d), heads_t, heads, pl.BlockSpec((1, 1, hv, tm), lambda i, j: (i, j, 0, 0)),
                   tok(ssm_w), tok(d), tok(d)],
        out_shape=[jax.ShapeDtypeStruct((b, s, d), F32),
                   jax.ShapeDtypeStruct((b, MLA_HEADS, HEAD_PAD, s), BF16),
                   jax.ShapeDtypeStruct((b, MLA_HEADS, s, HEAD_PAD), BF16),
                   jax.ShapeDtypeStruct((b, nt, hv, tm), BF16),
                   jax.ShapeDtypeStruct((b, s, ssm_w), BF16),
                   jax.ShapeDtypeStruct((b, s, d), BF16),
                   jax.ShapeDtypeStruct((b, s, d), BF16)],
        compiler_params=_cparams(("parallel", "parallel")),
        name="ffn_inproj",
    )(x, *ffn_args, mix_norm.reshape(1, d).astype(F32), win, q_norm.reshape(1, q_rank).astype(F32),
      kv_norm.reshape(1, kv_rank).astype(F32), wq, wk, wvt, rope_t, place, base, one)


def _attn_body(q_ref, k_ref, vt_ref, o_ref, s_ref, acc_ref, *, n_kt, tk):
    qt = q_ref[0, 0]
    tq = qt.shape[1]
    pack = 2 * SUBLANES
    first = lax.broadcasted_iota(jnp.int32, (pack, tq), 0) == 0

    def with_reference(m):
        row = jnp.where(first, jnp.broadcast_to(-m, (pack, tq)), 0.0).astype(BF16)
        return jnp.concatenate([qt[:AUG_LANE], row, qt[AUG_LANE + pack:]], axis=0)

    def rounded(m):
        return m.astype(BF16).astype(F32)

    def finish():
        acc = acc_ref[...]
        o_ref[0] = (acc[:V_HEAD_DIM] / acc[V_HEAD_DIM:V_HEAD_DIM + 1]).astype(BF16)

    s = _dot(k_ref[0, 0, 0:tk, :], qt)
    m = rounded(jnp.max(s, axis=0, keepdims=True))
    p_prev = jnp.exp2(s - m).astype(BF16)
    beta = jnp.ones_like(m)
    acc_ref[...] = jnp.zeros_like(acc_ref)
    gap = jnp.zeros_like(m)
    for j in range(1, n_kt):
        q_aug = with_reference(m)
        k_tile = k_ref[0, 0, j * tk:(j + 1) * tk, :]
        parts = []
        for i in range(tq // ATTN_SPLIT):
            cols = slice(i * ATTN_SPLIT, (i + 1) * ATTN_SPLIT)
            parts.append(jnp.exp2(_dot(k_tile, q_aug[:, cols])).astype(BF16))
            acc_ref[:, cols] = (acc_ref[:, cols] + _dot(vt_ref[0, j - 1], p_prev[:, cols])) * beta[:, cols]
        p = jnp.concatenate(parts, axis=1)
        g = jnp.log2(jnp.max(p, axis=0, keepdims=True).astype(F32))
        gap = jnp.maximum(gap, g)
        m_new = rounded(m + jnp.maximum(g, 0.0))
        beta = jnp.exp2(m - m_new)
        m = m_new
        p_prev = p
    acc_ref[...] = (acc_ref[...] + _dot(vt_ref[0, n_kt - 1], p_prev)) * beta
    finish()

    @pl.when(jnp.logical_not(jnp.max(gap) <= ATTN_MAX_GAP))
    def _():
        def tile(j, m):
            start = pl.multiple_of(j * tk, tk)
            s_ref[...] = _dot(k_ref[0, 0, pl.ds(start, tk), :], qt)
            s = s_ref[...]
            m_new = jnp.maximum(m, jnp.max(s, axis=0, keepdims=True))
            p = jnp.exp2(s - m_new).astype(BF16)
            acc_ref[...] = jnp.exp2(m - m_new) * acc_ref[...] + _dot(vt_ref[0, j], p)
            return m_new

        acc_ref[...] = jnp.zeros_like(acc_ref)
        lax.fori_loop(0, n_kt, tile, jnp.full((1, tq), -jnp.inf, F32))
        finish()


def _attention(q_t, k, vt):
    b, h, s, _ = k.shape
    n_kt, tk = vt.shape[1], vt.shape[3]
    vrows = vt.shape[2] // h
    tq = min(ATTN_TQ, s)
    assert tq % ATTN_SPLIT == 0
    return pl.pallas_call(
        functools.partial(_attn_body, n_kt=n_kt, tk=tk),
        grid=(b, h, s // tq),
        in_specs=[pl.BlockSpec((1, 1, HEAD_PAD, tq), lambda i, j, t: (i, j, 0, t)),
                  pl.BlockSpec((1, 1, s, HEAD_PAD), lambda i, j, t: (i, j, 0, 0)),
                  pl.BlockSpec((1, n_kt, vrows, tk), lambda i, j, t: (i, 0, j, 0))],
        out_specs=pl.BlockSpec((1, V_HEAD_DIM, tq), lambda i, j, t: (i, j, t)),
        out_shape=jax.ShapeDtypeStruct((b, h * V_HEAD_DIM, s), BF16),
        scratch_shapes=[pltpu.VMEM((tk, tq), F32), pltpu.VMEM((vrows, tq), F32)],
        compiler_params=_cparams(("parallel", "parallel", "arbitrary")),
        name="attention",
    )(q_t, k, vt)


def _discretise(lr, li, ldt):
    dt = jnp.exp(ldt)
    mag = jnp.exp(lr * dt)
    lb_re = mag * jnp.cos(li * dt)
    lb_im = mag * jnp.sin(li * dt)
    den = lr * lr + li * li
    nr = lb_re - 1.0
    return (nr * lr + lb_im * li) / den, (lb_im * lr - nr * li) / den, lr * dt, li * dt


def _ssm_prep_body(rowp_ref, colp_ref, bt_re_ref, bt_im_ref, bc_re_ref, bc_im_ref, cr_re_ref, cr_im_ref, dt_ref,
                   tt_ref, pp_ref, qq_ref, dec_ref, *, n_state):
    L = SSM_CHUNK
    rows = tt_ref.shape[1]
    width = rows // L
    lanes2 = 2 * n_state
    parity = pl.program_id(0) % 2
    lane = lax.broadcasted_iota(jnp.int32, (1, lanes2), 1)
    keep_lane = (lane >= parity * n_state) & (lane < (parity + 1) * n_state)
    first_lane = lane < n_state
    n_pow = (L + 1 + SUBLANES - 1) // SUBLANES * SUBLANES
    kpow = lax.broadcasted_iota(jnp.int32, (n_pow, 1), 0).astype(F32)
    rep = (lax.broadcasted_iota(jnp.int32, (rows + width, n_pow), 0) // width
           == lax.broadcasted_iota(jnp.int32, (rows + width, n_pow), 1)).astype(F32)
    sl = lax.broadcasted_iota(jnp.int32, (1, rows), 1) // width
    blk = lambda a, k: a[k * width:(k + 1) * width]

    k_rows = []
    for d in range(2):
        k_re, k_im, ar, ai = _discretise(rowp_ref[0, d, 0:1, :], rowp_ref[0, d, 1:2, :], rowp_ref[0, d, 2:3, :])
        magk = jnp.exp(kpow * ar)
        ak_re = _dot_exact(rep, magk * jnp.cos(kpow * ai))
        ak_im = _dot_exact(rep, magk * jnp.sin(kpow * ai))
        bt_re, bt_im = bt_re_ref[0, d], bt_im_ref[0, d]
        bb_re = k_re * bt_re - k_im * bt_im
        bb_im = k_re * bt_im + k_im * bt_re
        ba_re = bb_re * ak_re[:rows] - bb_im * ak_im[:rows]
        ba_im = bb_re * ak_im[:rows] + bb_im * ak_re[:rows]
        blocks = range(L - 1, -1, -1) if d == 0 else range(L)
        p_re = jnp.concatenate([blk(ba_re, k) for k in blocks], axis=0)
        p_im = jnp.concatenate([blk(ba_im, k) for k in blocks], axis=0)
        pp_ref[0, :, (2 * d) * lanes2:(2 * d + 1) * lanes2] = jnp.where(keep_lane, p_re, 0.0).astype(BF16)
        pp_ref[0, :, (2 * d + 1) * lanes2:(2 * d + 2) * lanes2] = jnp.where(keep_lane, p_im, 0.0).astype(BF16)
        cr_re, cr_im = cr_re_ref[0, d], cr_im_ref[0, d]
        ca_re = cr_re * ak_re - cr_im * ak_im
        ca_im = cr_re * ak_im + cr_im * ak_re
        blocks = range(1, L + 1) if d == 0 else range(L, 0, -1)
        q_re = jnp.concatenate([blk(ca_re, k) for k in blocks], axis=0)
        q_im = jnp.concatenate([blk(ca_im, k) for k in blocks], axis=0)
        qq_ref[0, :, (2 * d) * lanes2:(2 * d + 1) * lanes2] = jnp.where(keep_lane, q_re, 0.0).astype(BF16)
        qq_ref[0, :, (2 * d + 1) * lanes2:(2 * d + 2) * lanes2] = jnp.where(keep_lane, -q_im, 0.0).astype(BF16)
        kc_re, kc_im, _, _ = _discretise(colp_ref[0, d, :, 0:1], colp_ref[0, d, :, 1:2], colp_ref[0, d, :, 2:3])
        bc_re, bc_im = bc_re_ref[0, d], bc_im_ref[0, d]
        bbc_re = kc_re * bc_re - kc_im * bc_im
        bbc_im = kc_re * bc_im + kc_im * bc_re
        k_rows.append(_dot_exact(jnp.where(first_lane, ca_re[:rows], 0.0), bbc_re)
                      - _dot_exact(jnp.where(first_lane, ca_im[:rows], 0.0), bbc_im))
        dec_ref[0, 2 * d:2 * d + 1, :] = jnp.where(keep_lane, ak_re[rows:rows + 1], 0.0)
        dec_ref[0, 2 * d + 1:2 * d + 2, :] = jnp.where(keep_lane, ak_im[rows:rows + 1], 0.0)
    dec_ref[0, 4:8, :] = jnp.zeros((4, lanes2), F32)

    kf, kb = k_rows
    diag = blk(kf, 0) + blk(kb, 0) + dt_ref[0]
    for t in range(L):
        acc = jnp.where(sl == t, diag, 0.0)
        for k in range(1, t + 1):
            acc = jnp.where(sl == t - k, blk(kf, k), acc)
        for k in range(1, L - t):
            acc = jnp.where(sl == t + k, blk(kb, k), acc)
        tt_ref[0, t * width:(t + 1) * width, :] = acc.astype(BF16)


def _ssm_prep(lam_re, lam_im, log_dt, b_re, b_im, c_re, c_im, d_skip):
    _, g, p, w = b_re.shape
    L = SSM_CHUNK
    rows = L * w
    dup = lambda a, axis: jnp.concatenate([a, a], axis=axis)
    ldt = jnp.broadcast_to(log_dt[:, :, None], lam_re.shape)
    params = [dup(lam_re, -1), dup(lam_im, -1), dup(ldt, -1)]
    rowp = jnp.pad(jnp.stack(params, axis=2), ((0, 0), (0, 0), (0, SUBLANES - 3), (0, 0)))
    colp = jnp.pad(jnp.stack(params, axis=3), ((0, 0), (0, 0), (0, 0), (0, LANES - 3)))
    per_group = lambda a: jnp.swapaxes(a, 0, 1).astype(F32)
    bt = lambda b: jnp.tile(dup(jnp.swapaxes(b, 2, 3), -1), (1, 1, L, 1))
    bc = lambda b: jnp.tile(dup(b, 2), (1, 1, 1, L))
    cr = lambda c: jnp.tile(dup(c, -1), (1, 1, L + 1, 1))
    dmat = jnp.where(jnp.eye(w, dtype=bool)[None], d_skip.reshape(g, 1, w), 0.0)
    dtile = jnp.tile(dmat, (1, 1, L)).astype(F32)
    grp = lambda shape: pl.BlockSpec((1,) + shape, lambda i: (i,) + (0,) * len(shape))
    return pl.pallas_call(
        functools.partial(_ssm_prep_body, n_state=p),
        grid=(g,),
        in_specs=[grp((2, SUBLANES, 2 * p)), grp((2, 2 * p, LANES)), grp((2, rows, 2 * p)), grp((2, rows, 2 * p)),
                  grp((2, 2 * p, rows)), grp((2, 2 * p, rows)), grp((2, rows + w, 2 * p)), grp((2, rows + w, 2 * p)),
                  grp((w, rows))],
        out_specs=[grp((rows, rows)), grp((rows, 8 * p)), grp((rows, 8 * p)), grp((SUBLANES, 2 * p))],
        out_shape=[jax.ShapeDtypeStruct((g, rows, rows), BF16),
                   jax.ShapeDtypeStruct((g, rows, 8 * p), BF16),
                   jax.ShapeDtypeStruct((g, rows, 8 * p), BF16),
                   jax.ShapeDtypeStruct((g, SUBLANES, 2 * p), F32)],
        compiler_params=_cparams(("parallel",)),
        name="ssm_prep",
    )(per_group(rowp), per_group(colp), per_group(bt(b_re)), per_group(bt(b_im)), per_group(bc(b_re)),
      per_group(bc(b_im)), per_group(cr(c_re)), per_group(cr(c_im)), dtile)


def _utile_body(u_ref, eye_ref, o_ref, scr_ref):
    scr_ref[...] = u_ref[0].astype(F32)
    half = SSM_CHUNK // 2
    for sh in range(2):
        rows = jnp.concatenate([scr_ref[:, sh * half + s, :] for s in range(half)], axis=0).astype(BF16)
        o_ref[0, sh] = _dot_nt(eye_ref[...], rows).astype(BF16)


def _utile(u):
    b, s, w = u.shape
    L = SSM_CHUNK
    n_cb = s // (L * LANES)
    u4 = u.reshape(b, s // L, L, w)
    eye = jnp.eye(w, dtype=BF16)
    return pl.pallas_call(
        _utile_body,
        grid=(b, n_cb),
        in_specs=[pl.BlockSpec((1, LANES, L, w), lambda i, j: (i, j, 0, 0)),
                  pl.BlockSpec((w, w), lambda i, j: (0, 0))],
        out_specs=pl.BlockSpec((1, 2, w, LANES * L // 2), lambda i, j: (i, j, 0, 0)),
        out_shape=jax.ShapeDtypeStruct((b, 2 * n_cb, w, LANES * L // 2), BF16),
        scratch_shapes=[pltpu.VMEM((LANES, L, w), F32)],
        compiler_params=_cparams(("parallel", "parallel")),
        name="ssm_utile",
    )(u4, eye)


def _ssm_body(u_ref, tt_ref, pp_ref, qq_ref, dec_ref, y_ref, st_ref, *, n_chunks):
    bsz = u_ref.shape[0]
    n_cb = u_ref.shape[1] // 2
    width = u_ref.shape[2] // 2
    L = SSM_CHUNK
    half = L // 2
    w = LANES
    assert bsz == SUBLANES

    def tile_of(cb, step):
        return 2 * cb + step // half, slice((step % half) * w, (step % half + 1) * w)

    def ucol(b, g):
        pieces = []
        for s in range(L):
            row = []
            for cb in range(n_cb):
                tile, lanes = tile_of(cb, s)
                row.append(u_ref[b, tile, g * width:(g + 1) * width, lanes])
            pieces.append(jnp.concatenate(row, axis=1))
        return jnp.concatenate(pieces, axis=0)

    for b in range(bsz):
        e = _dot_tn(ucol(b, 0), pp_ref[0]) + _dot_tn(ucol(b, 1), pp_ref[1])
        for i in range(4):
            st_ref[i, pl.ds(b, n_chunks, stride=bsz), :] = e[:, i * w:(i + 1) * w]

    dec = dec_ref[0] + dec_ref[1]
    a_fr, a_fi, a_br, a_bi = (jnp.broadcast_to(dec[i:i + 1, :], (bsz, w)) for i in range(4))

    def step(c, carry):
        hfr, hfi, hbr, hbi = carry
        rf = pl.multiple_of(c * bsz, bsz)
        rb = pl.multiple_of((n_chunks - 1 - c) * bsz, bsz)
        efr = st_ref[0, pl.ds(rf, bsz), :]
        efi = st_ref[1, pl.ds(rf, bsz), :]
        ebr = st_ref[2, pl.ds(rb, bsz), :]
        ebi = st_ref[3, pl.ds(rb, bsz), :]
        st_ref[0, pl.ds(rf, bsz), :] = hfr
        st_ref[1, pl.ds(rf, bsz), :] = hfi
        st_ref[2, pl.ds(rb, bsz), :] = hbr
        st_ref[3, pl.ds(rb, bsz), :] = hbi
        return (a_fr * hfr - a_fi * hfi + efr, a_fi * hfr + a_fr * hfi + efi,
                a_br * hbr - a_bi * hbi + ebr, a_bi * hbr + a_br * hbi + ebi)

    zero = jnp.zeros((bsz, w), F32)
    lax.fori_loop(0, n_chunks, step, (zero, zero, zero, zero))

    for b in range(bsz):
        h = jnp.concatenate([st_ref[i, pl.ds(b, n_chunks, stride=bsz), :] for i in range(4)],
                            axis=1).astype(BF16)
        for g in range(2):
            y = (_dot(tt_ref[g], ucol(b, g)) + _dot_nt(qq_ref[g], h)).astype(BF16)
            for t in range(L):
                for cb in range(n_cb):
                    tile, lanes = tile_of(cb, t)
                    y_ref[b, tile, g * width:(g + 1) * width, lanes] = y[t * width:(t + 1) * width, cb * w:(cb + 1) * w]


def _ssm(u_t, tt, pp, qq, dec):
    b, n_tiles, ssm_w, tl = u_t.shape
    g, k, _ = tt.shape
    n_state8 = pp.shape[2]
    width = k // SSM_CHUNK
    n_chunks = n_tiles // 2 * LANES
    pair = lambda shape: pl.BlockSpec((2,) + shape, lambda i: (i,) + (0,) * len(shape))
    seq = pl.BlockSpec((b, n_tiles, 2 * width, tl), lambda i: (0, 0, i, 0))
    return pl.pallas_call(
        functools.partial(_ssm_body, n_chunks=n_chunks),
        grid=(g // 2,),
        in_specs=[seq, pair((k, k)), pair((k, n_state8)), pair((k, n_state8)), pair((SUBLANES, dec.shape[2]))],
        out_specs=seq,
        out_shape=jax.ShapeDtypeStruct(u_t.shape, BF16),
        scratch_shapes=[pltpu.VMEM((n_state8 // LANES, n_chunks * b, LANES), F32)],
        compiler_params=_cparams(("parallel",)),
        name="ssm",
    )(u_t, tt, pp, qq, dec)


def _ssm_post_body(y_ref, wglu_ref, bglu_ref, wos_ref, o_ref):
    ssm_w = y_ref.shape[2]
    y = y_ref[0, 0].astype(F32)
    glu = _dot_tn(jax.nn.gelu(y).astype(BF16), wglu_ref[...]) + bglu_ref[...]
    ssm_out = glu[:, :ssm_w] * jax.nn.sigmoid(glu[:, ssm_w:])
    out = _dot(ssm_out.astype(BF16), wos_ref[...])
    for s in range(SSM_CHUNK // 2):
        o_ref[0, :, s, :] = out[s * LANES:(s + 1) * LANES]


def _ssm_post(y_t, w_glu, b_glu, w_o_ssm):
    b, n_tiles, ssm_w, tl = y_t.shape
    d = w_o_ssm.shape[1]
    L = SSM_CHUNK
    s = n_tiles * tl
    const = lambda shape: pl.BlockSpec(shape, lambda i, j: (0,) * len(shape))
    out = pl.pallas_call(
        _ssm_post_body,
        grid=(b, n_tiles),
        in_specs=[pl.BlockSpec((1, 1, ssm_w, tl), lambda i, j: (i, j, 0, 0)),
                  const(w_glu.shape), const((1, w_glu.shape[1])), const(w_o_ssm.shape)],
        out_specs=pl.BlockSpec((1, LANES, L // 2, d), lambda i, j: (i, j // 2, j % 2, 0)),
        out_shape=jax.ShapeDtypeStruct((b, s // L, L, d), F32),
        compiler_params=_cparams(("parallel", "parallel")),
        name="ssm_post",
    )(y_t, w_glu.astype(BF16), b_glu.reshape(1, -1).astype(F32), w_o_ssm.astype(BF16))
    return out.reshape(b, s, d)


def _merge_ffn_body(h_ref, at_ref, ssm_ref, ga_ref, gs_ref, woa_ref, wout_ref, g_ref, wg_ref, wu_ref, wd_ref,
                    *rest):
    o_ref = rest[-1]
    attn = _dot_tn(at_ref[0], woa_ref[...])
    merged = ga_ref[0].astype(F32) * attn + gs_ref[0].astype(F32) * ssm_ref[0]
    h = h_ref[0] + _dot(merged.astype(BF16), wout_ref[...])
    h = _ffn_apply(h, g_ref, wg_ref, wu_ref, wd_ref)
    if len(rest) == 2:
        h = _rms(h, rest[0][...])
    o_ref[0] = h


def _merge_ffn(h1, attn_t, ssm, ga, gs, w_o_attn, w_out, norm, w_gate, w_up, w_down, final_norm=None):
    b, s, d = h1.shape
    hv = attn_t.shape[1]
    tm = min(TOKEN_TILE, s)
    tok = lambda w: pl.BlockSpec((1, tm, w), lambda i, j: (i, j, 0))
    ffn_args, ffn_specs = _ffn_operands(norm, w_gate, w_up, w_down)
    args = [h1, attn_t, ssm, ga, gs, w_o_attn.astype(BF16), w_out.astype(BF16)] + ffn_args
    in_specs = [tok(d), pl.BlockSpec((1, hv, tm), lambda i, j: (i, 0, j)), tok(d), tok(d), tok(d),
                _resident(w_o_attn.shape), _resident(w_out.shape)] + ffn_specs
    if final_norm is not None:
        args.append(final_norm.reshape(1, d).astype(F32))
        in_specs.append(_resident((1, d)))
    return pl.pallas_call(
        _merge_ffn_body,
        grid=(b, s // tm),
        in_specs=in_specs,
        out_specs=tok(d),
        out_shape=jax.ShapeDtypeStruct((b, s, d), F32),
        compiler_params=_cparams(("parallel", "parallel")),
        name="merge_ffn",
    )(*args)


def kernel(x, positions, ffn1_norm, ffn1_w_gate, ffn1_w_up, ffn1_w_down, mix_norm, w_in, q_norm, w_uq, kv_norm, w_ukv, w_o_attn, ssm_lambda_re, ssm_lambda_im, ssm_log_dt, ssm_b_re, ssm_b_im, ssm_c_re, ssm_c_im, ssm_d, w_glu, b_glu, w_o_ssm, w_out, ffn2_norm, ffn2_w_gate, ffn2_w_up, ffn2_w_down, final_norm):
    b, s, d = x.shape
    depth = ffn1_norm.shape[0]
    _, _, n_grp, _, grp_w = ssm_b_re.shape
    ssm_w = n_grp * grp_w
    L = SSM_CHUNK
    assert depth >= 1, "the final norm is fused into the last layer's second FFN"
    assert b == SUBLANES, "the S5 chunk recurrence keeps the batch on the 8 sublanes"
    assert s % max(TOKEN_TILE, ATTN_TQ, L * LANES) == 0 and L * grp_w == MXU_DIM and n_grp % 2 == 0

    rope_t = _rope_tables(positions)
    h = x
    for l in range(depth):
        last = l == depth - 1
        ffn1 = (ffn1_norm[l], ffn1_w_gate[l], ffn1_w_up[l], ffn1_w_down[l])
        h, q, k, vt, u, ga, gs = _ffn_inproj(h, ffn1, rope_t, mix_norm[l], w_in[l], q_norm[l], w_uq[l],
                                             kv_norm[l], w_ukv[l], ssm_w)
        attn_t = _attention(q, k, vt)

        tt, pp, qq, dec = _ssm_prep(ssm_lambda_re[l], ssm_lambda_im[l], ssm_log_dt[l], ssm_b_re[l], ssm_b_im[l],
                                    ssm_c_re[l], ssm_c_im[l], ssm_d[l])
        y_t = _ssm(_utile(u), tt, pp, qq, dec)
        ssm = _ssm_post(y_t, w_glu[l], b_glu[l], w_o_ssm[l])

        h = _merge_ffn(h, attn_t, ssm, ga, gs, w_o_attn[l], w_out[l], ffn2_norm[l], ffn2_w_gate[l],
                       ffn2_w_up[l], ffn2_w_down[l], final_norm if last else None)
    return h
```
